```python
import math
import jax, jax.numpy as jnp
from jax import lax
import numpy as np

D_MODEL = 2048
BATCH = 4
SEQ = 2048
DEPTH = 4
DEC_BATCH = 8
DEC_SEQ = 4
PAST_LEN = 16384
PAGE_SIZE = 128

H_A = 8
DH_A = 64
DK_A = 2 * DH_A
DV_A = 2 * DH_A
W_A = H_A * DV_A
H_M = 8
DM = 128
W_M = H_M * DM
MLSTM_CHUNK = 64
D_IN = 3 * W_A + 4 * W_M + 2 * H_M
SPLIT_POINTS = [W_A, 2 * W_A, 3 * W_A, 3 * W_A + W_M, 3 * W_A + 2 * W_M,
                3 * W_A + 3 * W_M, 3 * W_A + 4 * W_M, 3 * W_A + 4 * W_M + H_M]
D_FF = 5632
CONV_W = 3
N_BUCKETS = 32
MAX_DISTANCE = 128
Q_BLOCK = 128
LN_EPS = 1e-5
DEEPNORM_ALPHA = (2 * DEPTH) ** 0.25
DEEPNORM_BETA = (8 * DEPTH) ** -0.25

kernel_name = 'hymba_diffattn_mlstm_convffn_deepnorm_step'


def layer_norm(x, g, b):
    xf = x.astype(jnp.float32)
    mu = xf.mean(-1, keepdims=True)
    var = jnp.square(xf - mu).mean(-1, keepdims=True)
    return ((xf - mu) * lax.rsqrt(var + LN_EPS) * g + b).astype(x.dtype)


def rms_norm(x, w):
    xf = x.astype(jnp.float32)
    return xf * lax.rsqrt(jnp.square(xf).mean(-1, keepdims=True) + LN_EPS) * w


def rel_bucket(n):
    max_exact = N_BUCKETS // 2
    nf = jnp.maximum(n, 1).astype(jnp.float32)
    large = max_exact + (jnp.log(nf / max_exact) / math.log(MAX_DISTANCE / max_exact)
                         * (N_BUCKETS - max_exact)).astype(jnp.int32)
    large = jnp.minimum(large, N_BUCKETS - 1)
    return jnp.where(n < max_exact, n, large)


def diff_attention(q, k, v, q_off, lam, rel_bias):
    B, Lq = q.shape[:2]
    Lk = k.shape[1]
    qb = math.gcd(Lq, Q_BLOCK)
    nb = Lq // qb
    q_blocks = jnp.moveaxis(q.reshape(B, nb, qb, H_A, 2, DH_A), 1, 0)
    q_pos = (q_off + jnp.arange(Lq, dtype=jnp.int32)).reshape(nb, qb)
    k_pos = jnp.arange(Lk, dtype=jnp.int32)
    scale = DH_A ** -0.5

    def block(args):
        qblk, qp = args
        s = jnp.einsum('bqhcd,bkhcd->bhcqk', qblk, k, preferred_element_type=jnp.float32) * scale
        dist = qp[:, None] - k_pos[None, :]
        bias = jnp.moveaxis(rel_bias[rel_bucket(jnp.maximum(dist, 0))].astype(jnp.float32), -1, 0)
        s = jnp.where(dist >= 0, s + bias[None, :, None], -jnp.inf)
        p = jax.nn.softmax(s, axis=-1)
        a = p[:, :, 0] - lam * p[:, :, 1]
        return jnp.einsum('bhqk,bkhd->bqhd', a.astype(v.dtype), v)

    out = lax.map(block, (q_blocks, q_pos))
    return jnp.moveaxis(out, 0, 1).reshape(B, Lq, H_A, DV_A)


def mlstm_chunkwise(q, k, v, logi, logf, C0, n0, m0):
    B, H, L, d = q.shape
    cs = math.gcd(L, MLSTM_CHUNK)
    nc = L // cs

    def to_chunks(t):
        return jnp.moveaxis(t.astype(jnp.float32).reshape(B, H, nc, cs, *t.shape[3:]), 2, 0)

    xs = (to_chunks(q), to_chunks(k), to_chunks(v), to_chunks(logi), to_chunks(logf))
    causal = jnp.tril(jnp.ones((cs, cs), dtype=bool))

    def step(carry, inp):
        C, n, m = carry
        qc, kc, vc, li, lf = inp
        b = jnp.cumsum(lf, axis=-1)
        dmat = jnp.where(causal, b[..., :, None] - b[..., None, :] + li[..., None, :], -jnp.inf)
        inter = b + m[..., None]
        mt = jnp.maximum(dmat.max(-1), inter)
        w = jnp.exp(dmat - mt[..., None]) * jnp.einsum('bhtd,bhsd->bhts', qc, kc)
        a = jnp.exp(inter - mt)
        num = a[..., None] * jnp.einsum('bhvk,bhtk->bhtv', C, qc) + jnp.einsum('bhts,bhsv->bhtv', w, vc)
        den = a * jnp.einsum('bhk,bhtk->bht', n, qc) + w.sum(-1)
        h = num / jnp.maximum(jnp.abs(den), jnp.exp(-mt))[..., None]
        bL = b[..., -1]
        g = bL[..., None] - b + li
        m_new = jnp.maximum(bL + m, g.max(-1))
        wg = jnp.exp(g - m_new[..., None])
        decay = jnp.exp(bL + m - m_new)
        C_new = decay[..., None, None] * C + jnp.einsum('bhs,bhsv,bhsk->bhvk', wg, vc, kc)
        n_new = decay[..., None] * n + jnp.einsum('bhs,bhsk->bhk', wg, kc)
        return (C_new, n_new, m_new), h

    init = (C0.astype(jnp.float32), n0.astype(jnp.float32), m0.astype(jnp.float32))
    (C, n, m), h = lax.scan(step, init, xs)
    h = jnp.moveaxis(h, 0, 2).reshape(B, H, L, d)
    return h, C, n, m


def trunk_layer(x, past_k, past_v, q_off, C0, n0, m0, conv0, lam_init, rel_bias, p):
    B, L, _ = x.shape
    proj = x @ p['w_in']
    qa, ka, va, qm, km, vm, om, ig, fg = jnp.split(proj, SPLIT_POINTS, axis=-1)

    k_new = ka.reshape(B, L, H_A, DK_A)
    v_new = va.reshape(B, L, H_A, DV_A)
    if past_k is None:
        k_all, v_all = k_new, v_new
    else:
        k_all = jnp.concatenate([past_k.astype(k_new.dtype), k_new], axis=1)
        v_all = jnp.concatenate([past_v.astype(v_new.dtype), v_new], axis=1)
    f32 = jnp.float32
    lam = (jnp.exp(jnp.sum(p['lq1'].astype(f32) * p['lk1'].astype(f32)))
           - jnp.exp(jnp.sum(p['lq2'].astype(f32) * p['lk2'].astype(f32))) + lam_init)
    o_a = diff_attention(qa.reshape(B, L, H_A, 2, DH_A), k_all.reshape(B, -1, H_A, 2, DH_A),
                         v_all, q_off, lam, rel_bias)
    o_a = (rms_norm(o_a, p['subln_w']) * (1.0 - lam_init)).reshape(B, L, W_A)

    def heads(t):
        return t.reshape(B, L, H_M, DM).transpose(0, 2, 1, 3)
    logi = (ig + p['b_i']).astype(f32).transpose(0, 2, 1)
    logf = jax.nn.log_sigmoid((fg + p['b_f']).astype(f32)).transpose(0, 2, 1)
    h, C, n, m = mlstm_chunkwise(heads(qm), heads(km) * DM ** -0.5, heads(vm), logi, logf, C0, n0, m0)
    mu = h.mean(-1, keepdims=True)
    var = jnp.square(h - mu).mean(-1, keepdims=True)
    hn = ((h - mu) * lax.rsqrt(var + LN_EPS)).transpose(0, 2, 1, 3).reshape(B, L, W_M) * p['mhn_w']
    o_m = jax.nn.sigmoid((om + p['b_o']).astype(f32)) * hn

    mix = jnp.concatenate([o_a.astype(x.dtype), o_m.astype(x.dtype)], axis=-1) @ p['w_out']
    h1 = layer_norm(DEEPNORM_ALPHA * x + mix, p['ln1_g'], p['ln1_b'])

    up = h1 @ p['w_up']
    full = jnp.concatenate([conv0.astype(up.dtype), up], axis=1)
    conv = p['conv_b'] + sum(p['conv_w'][j] * full[:, j:j + L] for j in range(CONV_W))
    a_br, g_br = jnp.split(conv, 2, axis=-1)
    ff = (jax.nn.silu(g_br) * a_br) @ p['w_down']
    y = layer_norm(DEEPNORM_ALPHA * h1 + ff, p['ln2_g'], p['ln2_b'])
    conv_tail = full[:, L:]
    return y, k_new, v_new, C, n, m, conv_tail


def setup_inputs(seed: int = 0) -> dict:
    key = jax.random.key(seed)
    ks = iter(jax.random.split(key, 40))

    def nrm(shape, s):
        return jax.random.normal(next(ks), shape, jnp.float32) * s

    n_pages = PAST_LEN // PAGE_SIZE
    n_used = DEC_BATCH * n_pages
    n_pool = n_used + max(1, n_used // 4)
    page_table = jax.random.permutation(next(ks), n_pool)[:n_used].reshape(DEC_BATCH, n_pages).astype(jnp.int32)
    beta = DEEPNORM_BETA
    return {
        'x_prompt': nrm((BATCH, SEQ, D_MODEL), 1.0),
        'x_sample': nrm((DEC_BATCH, DEC_SEQ, D_MODEL), 1.0),
        'cache_k': nrm((DEPTH, n_pool, PAGE_SIZE, H_A, DK_A), 1.0),
        'cache_v': nrm((DEPTH, n_pool, PAGE_SIZE, H_A, DV_A), 1.0),
        'page_table': page_table,
        'state_C': nrm((DEPTH, DEC_BATCH, H_M, DM, DM), 0.1),
        'state_n': nrm((DEPTH, DEC_BATCH, H_M, DM), 0.1),
        'state_m': nrm((DEPTH, DEC_BATCH, H_M), 1.0),
        'state_conv': nrm((DEPTH, DEC_BATCH, CONV_W - 1, 2 * D_FF), 1.0),
        'rel_bias': nrm((N_BUCKETS, H_A), 0.5),
        'w_in': nrm((DEPTH, D_MODEL, D_IN), D_MODEL ** -0.5),
        'b_i': nrm((DEPTH, H_M), 0.1),
        'b_f': jnp.linspace(3.0, 6.0, H_M, dtype=jnp.float32)[None, :] + nrm((DEPTH, H_M), 0.1),
        'b_o': nrm((DEPTH, W_M), 0.1),
        'lambda_q1': nrm((DEPTH, DH_A), 0.1),
        'lambda_k1': nrm((DEPTH, DH_A), 0.1),
        'lambda_q2': nrm((DEPTH, DH_A), 0.1),
        'lambda_k2': nrm((DEPTH, DH_A), 0.1),
        'subln_w': 1.0 + nrm((DEPTH, DV_A), 0.1),
        'mhn_w': 1.0 + nrm((DEPTH, W_M), 0.1),
        'w_out': nrm((DEPTH, W_A + W_M, D_MODEL), (W_A + W_M) ** -0.5 * beta),
        'ln1_g': 1.0 + nrm((DEPTH, D_MODEL), 0.1),
        'ln1_b': nrm((DEPTH, D_MODEL), 0.02),
        'w_up': nrm((DEPTH, D_MODEL, 2 * D_FF), D_MODEL ** -0.5),
        'conv_w': nrm((DEPTH, CONV_W, 2 * D_FF), CONV_W ** -0.5),
        'conv_b': nrm((DEPTH, 2 * D_FF), 0.02),
        'w_down': nrm((DEPTH, D_FF, D_MODEL), D_FF ** -0.5 * beta),
        'ln2_g': 1.0 + nrm((DEPTH, D_MODEL), 0.1),
        'ln2_b': nrm((DEPTH, D_MODEL), 0.02),
    }


def reference(x_prompt, x_sample, cache_k, cache_v, page_table, state_C, state_n, state_m, state_conv,
              rel_bias, w_in, b_i, b_f, b_o, lambda_q1, lambda_k1, lambda_q2, lambda_k2, subln_w, mhn_w,
              w_out, ln1_g, ln1_b, w_up, conv_w, conv_b, w_down, ln2_g, ln2_b):
    xp, xs = x_prompt, x_sample
    f32 = jnp.float32
    zC = jnp.zeros((BATCH, H_M, DM, DM), f32)
    zn = jnp.zeros((BATCH, H_M, DM), f32)
    zm = jnp.zeros((BATCH, H_M), f32)
    zconv = jnp.zeros((BATCH, CONV_W - 1, 2 * D_FF), x_prompt.dtype)
    kp, vp, ksm, vsm = [], [], [], []
    Cp, np_, mp, Cs, ns, ms = [], [], [], [], [], []
    convp, convs = [], []
    for l in range(DEPTH):
        lam_init = 0.8 - 0.6 * math.exp(-0.3 * l)
        p = {'w_in': w_in[l], 'b_i': b_i[l], 'b_f': b_f[l], 'b_o': b_o[l],
             'lq1': lambda_q1[l], 'lk1': lambda_k1[l], 'lq2': lambda_q2[l], 'lk2': lambda_k2[l],
             'subln_w': subln_w[l], 'mhn_w': mhn_w[l], 'w_out': w_out[l],
             'ln1_g': ln1_g[l], 'ln1_b': ln1_b[l], 'w_up': w_up[l], 'conv_w': conv_w[l],
             'conv_b': conv_b[l], 'w_down': w_down[l], 'ln2_g': ln2_g[l], 'ln2_b': ln2_b[l]}
        xp, k1, v1, C1, n1, m1, c1 = trunk_layer(xp, None, None, 0, zC, zn, zm, zconv, lam_init, rel_bias, p)
        past_k = cache_k[l][page_table].reshape(DEC_BATCH, PAST_LEN, H_A, DK_A)
        past_v = cache_v[l][page_table].reshape(DEC_BATCH, PAST_LEN, H_A, DV_A)
        xs, k2, v2, C2, n2, m2, c2 = trunk_layer(xs, past_k, past_v, PAST_LEN, state_C[l], state_n[l],
                                                 state_m[l], state_conv[l], lam_init, rel_bias, p)
        kp.append(k1); vp.append(v1); ksm.append(k2); vsm.append(v2)
        Cp.append(C1); np_.append(n1); mp.append(m1)
        Cs.append(C2); ns.append(n2); ms.append(m2)
        convp.append(c1); convs.append(c2)
    return (xp, xs,
            jnp.stack(kp), jnp.stack(vp), jnp.stack(ksm), jnp.stack(vsm),
            jnp.stack(Cp), jnp.stack(np_), jnp.stack(mp),
            jnp.stack(Cs), jnp.stack(ns), jnp.stack(ms),
            jnp.stack(convp), jnp.stack(convs))
```

```python
import functools
import math

import numpy as np
import jax
import jax.numpy as jnp
from jax import lax
from jax.experimental import pallas as pl
from jax.experimental.pallas import tpu as pltpu

F32 = jnp.float32
BF16 = jnp.bfloat16

LN_EPS = 1e-5
N_BUCKETS = 32
MAX_DISTANCE = 128
CONV_W = 3
NEG = -1e30
SAMPLE_ROWS = 8
V7X_VMEM_LIMIT = 56 * 1024 * 1024


def _bucket_thresholds():
    max_exact = N_BUCKETS // 2
    d = np.arange(0, 4 * MAX_DISTANCE)
    large = max_exact + (np.log(np.maximum(d, 1) / max_exact) / math.log(MAX_DISTANCE / max_exact)
                         * (N_BUCKETS - max_exact)).astype(np.int64)
    bucket = np.where(d < max_exact, d, np.minimum(large, N_BUCKETS - 1))
    return tuple(int(d[bucket >= k].min()) for k in range(1, N_BUCKETS))


_THR = _bucket_thresholds()
_FAR = _THR[-1]


def _cparams(sem):
    return pltpu.CompilerParams(dimension_semantics=sem, vmem_limit_bytes=V7X_VMEM_LIMIT)


def _nt(a, b):
    return lax.dot_general(a, b, (((1,), (1,)), ((), ())), preferred_element_type=F32)


def _tn(a, b):
    return lax.dot_general(a, b, (((0,), (0,)), ((), ())), preferred_element_type=F32)


def _sigmoid(x):
    return 1.0 / (1.0 + jnp.exp(-x))


def _pbias_kernel(rb_ref, o_ref, *, T):
    h = pl.program_id(0)
    r = lax.broadcasted_iota(jnp.int32, (T, T), 0)
    c = lax.broadcasted_iota(jnp.int32, (T, T), 1)
    for t, off in enumerate((0, T)):
        d = r - c + off
        bias = jnp.full((T, T), rb_ref[0, h], F32)
        for k in range(1, N_BUCKETS):
            bias = jnp.where(d >= _THR[k - 1], rb_ref[k, h], bias)
        if off == 0:
            bias = jnp.where(d >= 0, bias, NEG)
        o_ref[0, t] = bias


def _prompt_bias(rel_bias, T):
    H = rel_bias.shape[1]
    return pl.pallas_call(
        functools.partial(_pbias_kernel, T=T),
        grid=(H,),
        in_specs=[pl.BlockSpec(memory_space=pltpu.SMEM)],
        out_specs=pl.BlockSpec((1, 2, T, T), lambda h: (h, 0, 0, 0)),
        out_shape=jax.ShapeDtypeStruct((H, 2, T, T), F32),
        compiler_params=_cparams(("arbitrary",)),
        name="prompt_bias",
    )(rel_bias)


def _dbias_kernel(rb_ref, d_ref, n_ref, *, P, G):
    h = pl.program_id(0)
    NQ = SAMPLE_ROWS
    R = 2 * NQ

    def build(d):
        bias = jnp.full(d.shape, rb_ref[0, h], F32)
        for k in range(1, N_BUCKETS):
            bias = jnp.where(d >= _THR[k - 1], rb_ref[k, h], bias)
        return bias

    r = lax.broadcasted_iota(jnp.int32, (R, P), 0)
    c = lax.broadcasted_iota(jnp.int32, (R, P), 1)
    qi = r % NQ
    far = jnp.full((R, P), rb_ref[N_BUCKETS - 1, h], F32)
    last = build(P + qi - c)
    dn = qi - c
    new = jnp.where(dn >= 0, build(jnp.maximum(dn, 0)), NEG)
    d_ref[0, 0] = jnp.concatenate([far] * G, axis=1)
    d_ref[0, 1] = jnp.concatenate([far] * (G - 1) + [last], axis=1)
    n_ref[0] = new


def _decode_bias(rel_bias, P, G):
    H = rel_bias.shape[1]
    R = 2 * SAMPLE_ROWS
    return pl.pallas_call(
        functools.partial(_dbias_kernel, P=P, G=G),
        grid=(H,),
        in_specs=[pl.BlockSpec(memory_space=pltpu.SMEM)],
        out_specs=[pl.BlockSpec((1, 2, R, G * P), lambda h: (h, 0, 0, 0)),
                   pl.BlockSpec((1, R, P), lambda h: (h, 0, 0))],
        out_shape=(jax.ShapeDtypeStruct((H, 2, R, G * P), F32), jax.ShapeDtypeStruct((H, R, P), F32)),
        compiler_params=_cparams(("arbitrary",)),
        name="decode_bias",
    )(rel_bias)


def _inproj_kernel(x_ref, w_ref, wg_ref, pb_ref, k_ref, v_ref, om_ref, g_ref, *, H, NG):
    j = pl.program_id(1)
    x = x_ref[...].astype(BF16)
    acc = jnp.dot(x, w_ref[...], preferred_element_type=F32)
    bm, W = acc.shape
    DK = W // H

    def store_heads(ref):
        for h in range(H):
            ref[pl.ds(h, bm, stride=H), :] = acc[:, h * DK:(h + 1) * DK]

    @pl.when(j <= 5)
    def _():
        pb_ref[...] = acc.astype(pb_ref.dtype)

    @pl.when(j == 1)
    def _():
        store_heads(k_ref)

    @pl.when(j == 2)
    def _():
        store_heads(v_ref)

    @pl.when(j == 6)
    def _():
        om_ref[...] = acc

    @pl.when(j == 0)
    def _():
        gates = jnp.dot(x, wg_ref[...], preferred_element_type=F32)
        if bm % 128:
            gates = jnp.concatenate([gates, jnp.zeros((128 - bm % 128, gates.shape[1]), F32)], axis=0)
        g_ref[0] = gates.T[:NG, :bm]


def _inproj(x, w_main, wg, *, bm, rows_per_seq, pb_dtype, H, NG):
    M, D = x.shape
    W = w_main.shape[1] // 7
    DK = W // H
    tpb = rows_per_seq // bm
    grid = (M // bm, 7)
    return pl.pallas_call(
        functools.partial(_inproj_kernel, H=H, NG=NG),
        grid=grid,
        in_specs=[pl.BlockSpec((bm, D), lambda i, j: (i, 0)),
                  pl.BlockSpec((D, W), lambda i, j: (0, j)),
                  pl.BlockSpec(wg.shape, lambda i, j: (0, 0))],
        out_specs=[pl.BlockSpec((bm, W), lambda i, j: (i, jnp.minimum(j, 5))),
                   pl.BlockSpec((bm * H, DK), lambda i, j: (i, 0)),
                   pl.BlockSpec((bm * H, DK), lambda i, j: (i, 0)),
                   pl.BlockSpec((bm, W), lambda i, j: (i, 0)),
                   pl.BlockSpec((1, NG, bm), lambda i, j: (i // tpb, 0, i % tpb))],
        out_shape=[jax.ShapeDtypeStruct((M, 6 * W), pb_dtype),
                   jax.ShapeDtypeStruct((M * H, DK), F32),
                   jax.ShapeDtypeStruct((M * H, DK), F32),
                   jax.ShapeDtypeStruct((M, W), F32),
                   jax.ShapeDtypeStruct((M // rows_per_seq, NG, rows_per_seq), F32)],
        compiler_params=_cparams(("arbitrary", "arbitrary")),
        name="in_proj",
    )(x, w_main, wg)


def _mm_pair_kernel(x1_ref, x2_ref, w1_ref, w2_ref, o_ref):
    o_ref[...] = (jnp.dot(x1_ref[...].astype(BF16), w1_ref[...], preferred_element_type=F32)
                  + jnp.dot(x2_ref[...].astype(BF16), w2_ref[...], preferred_element_type=F32))


def _mm_pair(x1, x2, w, *, bm, bn):
    M, K1 = x1.shape
    N = w.shape[1]
    return pl.pallas_call(
        _mm_pair_kernel,
        grid=(N // bn, M // bm),
        in_specs=[pl.BlockSpec((bm, K1), lambda j, i: (i, 0)),
                  pl.BlockSpec((bm, K1), lambda j, i: (i, 0)),
                  pl.BlockSpec((K1, bn), lambda j, i: (0, j)),
                  pl.BlockSpec((K1, bn), lambda j, i: (1, j))],
        out_specs=pl.BlockSpec((bm, bn), lambda j, i: (i, j)),
        out_shape=jax.ShapeDtypeStruct((M, N), F32),
        compiler_params=_cparams(("arbitrary", "arbitrary")),
        name="out_proj",
    )(x1, x2, w, w)


def _mm_k_kernel(x_ref, w_ref, o_ref):
    k = pl.program_id(2)
    d = jnp.dot(x_ref[...].astype(BF16), w_ref[...], preferred_element_type=F32)

    @pl.when(k == 0)
    def _():
        o_ref[...] = d

    @pl.when(k > 0)
    def _():
        o_ref[...] += d


def _mm_k(x, w, *, bm, bn, bk):
    M, K = x.shape
    N = w.shape[1]
    return pl.pallas_call(
        _mm_k_kernel,
        grid=(M // bm, N // bn, K // bk),
        in_specs=[pl.BlockSpec((bm, bk), lambda i, j, k: (i, k)),
                  pl.BlockSpec((bk, bn), lambda i, j, k: (k, j))],
        out_specs=pl.BlockSpec((bm, bn), lambda i, j, k: (i, j)),
        out_shape=jax.ShapeDtypeStruct((M, N), F32),
        compiler_params=_cparams(("arbitrary", "arbitrary", "arbitrary")),
        name="down_proj",
    )(x, w)


def _ln_kernel(x_ref, y_ref, g_ref, b_ref, o32_ref, o16_ref, *, alpha):
    z = alpha * x_ref[...] + y_ref[...]
    mu = jnp.mean(z, axis=-1, keepdims=True)
    zc = z - mu
    var = jnp.mean(zc * zc, axis=-1, keepdims=True)
    o = zc * lax.rsqrt(var + LN_EPS) * g_ref[...] + b_ref[...]
    o32_ref[...] = o
    o16_ref[...] = o.astype(BF16)


def _ln_res(x, y, g, b, *, alpha, tm):
    M, D = x.shape
    row = pl.BlockSpec((tm, D), lambda i: (i, 0))
    vec = pl.BlockSpec((1, D), lambda i: (0, 0))
    return pl.pallas_call(
        functools.partial(_ln_kernel, alpha=alpha),
        grid=(M // tm,),
        in_specs=[row, row, vec, vec],
        out_specs=[row, row],
        out_shape=[jax.ShapeDtypeStruct((M, D), F32), jax.ShapeDtypeStruct((M, D), BF16)],
        compiler_params=_cparams(("arbitrary",)),
        name="res_layernorm",
    )(x, y, g.reshape(1, D), b.reshape(1, D))


def _upconv_kernel(x_ref, wa_ref, wg_ref, cwa_ref, cwg_ref, cba_ref, cbg_ref, c0a_ref, c0g_ref,
                   act_ref, ta_ref, tg_ref, ha_ref, hg_ref, *, bm, nmt):
    i = pl.program_id(2)
    x = x_ref[...].astype(BF16)
    rid = lax.broadcasted_iota(jnp.int32, (8, act_ref.shape[1]), 0)
    conv = []
    for w_ref, cw_ref, cb_ref, c0_ref, t_ref, h_ref in ((wa_ref, cwa_ref, cba_ref, c0a_ref, ta_ref, ha_ref),
                                                        (wg_ref, cwg_ref, cbg_ref, c0g_ref, tg_ref, hg_ref)):
        u = jnp.dot(x, w_ref[...], preferred_element_type=F32)
        c0 = c0_ref[0]
        prev = h_ref[...]
        first = i == 0
        p2 = jnp.where(first, c0[0:1], prev[6:7])
        p1 = jnp.where(first, c0[1:2], prev[7:8])
        u1 = pltpu.roll(u, 1, axis=0)
        u2 = pltpu.roll(u, 2, axis=0)
        h1 = jnp.where(rid == 0, p1, u1[:8])
        h2 = jnp.where(rid == 0, p2, jnp.where(rid == 1, p1, u2[:8]))
        if bm > 8:
            u1 = jnp.concatenate([h1, u1[8:]], axis=0)
            u2 = jnp.concatenate([h2, u2[8:]], axis=0)
        else:
            u1, u2 = h1, h2
        cw = cw_ref[...]
        conv.append(cb_ref[...] + cw[0:1] * u2 + cw[1:2] * u1 + cw[2:3] * u)
        h_ref[...] = u[bm - 8:]

        @pl.when(i == nmt - 1)
        def _():
            t_ref[0] = u[bm - 8:]

    a_br, g_br = conv
    act_ref[...] = (g_br * _sigmoid(g_br) * a_br).astype(act_ref.dtype)


def _upconv(x, w_up, conv_w, conv_b, conv0, *, bm, bn, rows_per_seq, act_dtype):
    M, D = x.shape
    F = w_up.shape[1] // 2
    NT = F // bn
    nb = M // rows_per_seq
    nmt = rows_per_seq // bm
    cb = conv_b.reshape(1, 2 * F)
    half = lambda off: (lambda j, b, i: (0, off + j))
    st = lambda off: (lambda j, b, i: (b, 0, off + j))
    return pl.pallas_call(
        functools.partial(_upconv_kernel, bm=bm, nmt=nmt),
        grid=(NT, nb, nmt),
        in_specs=[pl.BlockSpec((bm, D), lambda j, b, i: (b * nmt + i, 0)),
                  pl.BlockSpec((D, bn), half(0)), pl.BlockSpec((D, bn), half(NT)),
                  pl.BlockSpec((CONV_W, bn), half(0)), pl.BlockSpec((CONV_W, bn), half(NT)),
                  pl.BlockSpec((1, bn), half(0)), pl.BlockSpec((1, bn), half(NT)),
                  pl.BlockSpec((1, CONV_W - 1, bn), st(0)), pl.BlockSpec((1, CONV_W - 1, bn), st(NT))],
        out_specs=[pl.BlockSpec((bm, bn), lambda j, b, i: (b * nmt + i, j)),
                   pl.BlockSpec((1, 8, bn), st(0)), pl.BlockSpec((1, 8, bn), st(0))],
        out_shape=[jax.ShapeDtypeStruct((M, F), act_dtype),
                   jax.ShapeDtypeStruct((nb, 8, F), F32), jax.ShapeDtypeStruct((nb, 8, F), F32)],
        scratch_shapes=[pltpu.VMEM((8, bn), F32), pltpu.VMEM((8, bn), F32)],
        compiler_params=_cparams(("arbitrary", "arbitrary", "arbitrary")),
        name="up_conv_gate",
    )(x, w_up, w_up, conv_w, conv_w, cb, cb, conv0, conv0)


def _lambda(lv_ref, lam_init):
    lv = lv_ref[...]
    s1 = jnp.sum(lv[0:1] * lv[1:2], axis=1, keepdims=True)
    s2 = jnp.sum(lv[2:3] * lv[3:4], axis=1, keepdims=True)
    return jnp.exp(s1) - jnp.exp(s2) + lam_init


def _pattn_kernel(li_ref, rb_ref, q_ref, k_ref, v_ref, pbias_ref, lv_ref, sw_ref, o_ref,
                  m_ref, l_ref, acc_ref, *, T, DH):
    h = pl.program_id(1)
    i = pl.program_id(2)
    scale = DH ** -0.5
    q = q_ref[...]
    qs = [q[:, :DH], q[:, DH:]]
    m_ref[...] = jnp.full(m_ref.shape, NEG, F32)
    l_ref[...] = jnp.zeros(l_ref.shape, F32)
    acc_ref[...] = jnp.zeros(acc_ref.shape, F32)

    def tile(j, bias):
        start = pl.multiple_of(j * T, T)
        kt = k_ref[pl.ds(start, T), :]
        vt = v_ref[pl.ds(start, T), :]
        for mp in range(2):
            s = _nt(qs[mp], kt[:, mp * DH:(mp + 1) * DH]) * scale + bias
            m_old = m_ref[mp]
            m_new = jnp.maximum(m_old, jnp.max(s, axis=1, keepdims=True))
            alpha = jnp.exp(m_old - m_new)
            p = jnp.exp(s - m_new)
            l_ref[mp] = alpha * l_ref[mp] + jnp.sum(p, axis=1, keepdims=True)
            acc_ref[mp] = alpha * acc_ref[mp] + jnp.dot(p.astype(BF16), vt, preferred_element_type=F32)
            m_ref[mp] = m_new

    far_bias = rb_ref[N_BUCKETS - 1, h]

    def far_body(j, carry):
        tile(j, far_bias)
        return carry

    lax.fori_loop(0, jnp.maximum(i - 1, 0), far_body, 0)

    @pl.when(i >= 1)
    def _():
        tile(i - 1, pbias_ref[0, 1])

    tile(i, pbias_ref[0, 0])

    lam_init = li_ref[0]
    lam = _lambda(lv_ref, lam_init)
    o = acc_ref[0] / l_ref[0] - lam * (acc_ref[1] / l_ref[1])
    o = o * lax.rsqrt(jnp.mean(o * o, axis=-1, keepdims=True) + LN_EPS) * sw_ref[...] * (1.0 - lam_init)
    o_ref[...] = o.astype(o_ref.dtype)


def _prompt_attention(pb, rel_bias, pbias, lam_vecs, subln_w, lam_init, *, B, L, H, DH, T):
    M = pb.shape[0]
    DV = 2 * DH
    nq = L // T
    return pl.pallas_call(
        functools.partial(_pattn_kernel, T=T, DH=DH),
        grid=(B, H, nq),
        in_specs=[pl.BlockSpec(memory_space=pltpu.SMEM),
                  pl.BlockSpec(memory_space=pltpu.SMEM),
                  pl.BlockSpec((T, DV), lambda b, h, i: (b * nq + i, h)),
                  pl.BlockSpec((L, DV), lambda b, h, i: (b, H + h)),
                  pl.BlockSpec((L, DV), lambda b, h, i: (b, 2 * H + h)),
                  pl.BlockSpec((1, 2, T, T), lambda b, h, i: (h, 0, 0, 0)),
                  pl.BlockSpec((4, DH), lambda b, h, i: (0, 0)),
                  pl.BlockSpec((1, DV), lambda b, h, i: (0, 0))],
        out_specs=pl.BlockSpec((T, DV), lambda b, h, i: (b * nq + i, h)),
        out_shape=jax.ShapeDtypeStruct((M, H * DV), BF16),
        scratch_shapes=[pltpu.VMEM((2, T, 1), F32), pltpu.VMEM((2, T, 1), F32), pltpu.VMEM((2, T, DV), F32)],
        compiler_params=_cparams(("arbitrary", "arbitrary", "arbitrary")),
        name="prompt_attention",
    )(lam_init, rel_bias, pb, pb, pb, pbias, lam_vecs, subln_w.reshape(1, DV))


def _dattn_kernel(pt_ref, li_ref, q_ref, kn_ref, vn_ref, db_ref, nb_ref, lv_ref, sw_ref, *rest,
                  G, P, H, DH, nsteps):
    del pt_ref
    k_refs = rest[:G]
    v_refs = rest[G:2 * G]
    o_ref, qb_ref, m_ref, l_ref, acc_ref = rest[2 * G:]
    NQ = SAMPLE_ROWS
    DV = 2 * DH
    s_id = pl.program_id(1)

    @pl.when(s_id == 0)
    def _():
        lane = lax.broadcasted_iota(jnp.int32, (NQ, DV), 1)
        for h in range(H):
            qh = q_ref[:, h * DV:(h + 1) * DV] * (DH ** -0.5)
            qb_ref[h] = jnp.concatenate([jnp.where(lane < DH, qh, 0.0), jnp.where(lane >= DH, qh, 0.0)],
                                        axis=0).astype(BF16)
        m_ref[...] = jnp.full(m_ref.shape, NEG, F32)
        l_ref[...] = jnp.zeros(l_ref.shape, F32)
        acc_ref[...] = jnp.zeros(acc_ref.shape, F32)

    def update(h, kh, vh, bias):
        s = _nt(qb_ref[h], kh) + bias
        m_old = m_ref[h]
        m_new = jnp.maximum(m_old, jnp.max(s, axis=1, keepdims=True))
        alpha = jnp.exp(m_old - m_new)
        p = jnp.exp(s - m_new)
        l_ref[h] = alpha * l_ref[h] + jnp.sum(p, axis=1, keepdims=True)
        acc_ref[h] = alpha * acc_ref[h] + jnp.dot(p.astype(BF16), vh, preferred_element_type=F32)
        m_ref[h] = m_new

    is_last = (s_id == nsteps - 1).astype(jnp.int32)
    for h in range(H):
        kh = jnp.concatenate([r[pl.ds(h, P, stride=H), :] for r in k_refs], axis=0).astype(BF16)
        vh = jnp.concatenate([r[pl.ds(h, P, stride=H), :] for r in v_refs], axis=0).astype(BF16)
        update(h, kh, vh, db_ref[h, is_last])

    @pl.when(s_id == nsteps - 1)
    def _():
        lam_init = li_ref[0]
        lam = _lambda(lv_ref, lam_init)
        pad = jnp.zeros((P - NQ, DV), F32)
        for h in range(H):
            sl = slice(h * DV, (h + 1) * DV)
            kn = jnp.concatenate([kn_ref[:, sl], pad], axis=0).astype(BF16)
            vn = jnp.concatenate([vn_ref[:, sl], pad], axis=0).astype(BF16)
            update(h, kn, vn, nb_ref[h])
            a = acc_ref[h] / l_ref[h]
            o = a[:NQ] - lam * a[NQ:]
            o = o * lax.rsqrt(jnp.mean(o * o, axis=-1, keepdims=True) + LN_EPS) * sw_ref[...] * (1.0 - lam_init)
            o_ref[:, sl] = o


def _decode_attention(pb_s, cache_k, cache_v, page_idx, dbias, nbias, lam_vecs, subln_w, lam_init,
                      *, BS, NP, G, P, H, DH):
    NQ = SAMPLE_ROWS
    DV = 2 * DH
    WA = H * DV
    R = 2 * NQ
    nsteps = NP // G
    row = lambda col: pl.BlockSpec((NQ, WA), lambda b, s, pt: (b, col))
    full = lambda shape: pl.BlockSpec(shape, lambda b, s, pt: (0,) * len(shape))
    page = lambda g: pl.BlockSpec((None, P * H, DV), lambda b, s, pt: (pt[b * NP + s * G + g], 0, 0))
    grid_spec = pltpu.PrefetchScalarGridSpec(
        num_scalar_prefetch=1,
        grid=(BS, nsteps),
        in_specs=[pl.BlockSpec(memory_space=pltpu.SMEM),
                  row(0), row(1), row(2),
                  full((H, 2, R, G * P)), full((H, R, P)), full((4, DH)), full((1, DV))]
                 + [page(g) for g in range(G)] + [page(g) for g in range(G)],
        out_specs=pl.BlockSpec((NQ, WA), lambda b, s, pt: (b, 0)),
        scratch_shapes=[pltpu.VMEM((H, R, DV), BF16), pltpu.VMEM((H, R, 1), F32), pltpu.VMEM((H, R, 1), F32),
                        pltpu.VMEM((H, R, DV), F32)])
    return pl.pallas_call(
        functools.partial(_dattn_kernel, G=G, P=P, H=H, DH=DH, nsteps=nsteps),
        grid_spec=grid_spec,
        out_shape=jax.ShapeDtypeStruct((BS * NQ, WA), F32),
        compiler_params=_cparams(("arbitrary", "arbitrary")),
        name="decode_attention",
    )(page_idx, lam_init, pb_s, pb_s, pb_s, dbias, nbias, lam_vecs, subln_w.reshape(1, DV),
      *([cache_k] * G), *([cache_v] * G))


def _log_sigmoid(x):
    return jnp.minimum(x, 0.0) - jnp.log1p(jnp.exp(-jnp.abs(x)))


def _mlstm_kernel(*refs, cs, rows, valid, zero_init, H, DM, nc):
    if zero_init:
        g_ref, gb_ref, q_ref, k_ref, v_ref, om_ref, bo_ref, mw_ref = refs[:8]
        rest = refs[8:]
    else:
        g_ref, gb_ref, q_ref, k_ref, v_ref, om_ref, bo_ref, mw_ref, c0_ref, n0_ref, m0_ref = refs[:11]
        rest = refs[11:]
    o_ref, co_ref, no_ref, mo_ref, c_s, n_s, m_s = rest
    c = pl.program_id(1)
    scale = DM ** -0.5

    @pl.when(c == 0)
    def _():
        if zero_init:
            c_s[...] = jnp.zeros(c_s.shape, F32)
            n_s[...] = jnp.zeros(n_s.shape, F32)
            m_s[...] = jnp.zeros(m_s.shape, F32)
        else:
            c_s[...] = c0_ref[0]
            n_s[...] = n0_ref[0]
            m_s[...] = m0_ref[0]

    g = g_ref[0] + gb_ref[...]
    li_all = g[:H]
    lf_all = _log_sigmoid(g[H:])
    if valid < cs:
        col = lax.broadcasted_iota(jnp.int32, (H, cs), 1)
        li_all = jnp.where(col < valid, li_all, NEG)
        lf_all = jnp.where(col < valid, lf_all, 0.0)
    rr = lax.broadcasted_iota(jnp.int32, (cs, cs), 0)
    cc = lax.broadcasted_iota(jnp.int32, (cs, cs), 1)
    causal = cc <= rr
    eye = cc == rr
    upper = (rr <= cc).astype(F32)
    b_all = jnp.dot(lf_all, upper, preferred_element_type=F32, precision=lax.Precision.HIGHEST)

    def pad_rows(x):
        if rows == cs:
            return x
        return jnp.concatenate([x, jnp.zeros((cs - rows, x.shape[1]), x.dtype)], axis=0)

    def to_col(row):
        return jnp.sum(jnp.where(eye, row, 0.0), axis=1, keepdims=True)

    for h in range(H):
        sl = slice(h * DM, (h + 1) * DM)
        q = pad_rows(q_ref[:, sl])
        k = pad_rows(k_ref[:, sl])
        v = pad_rows(v_ref[:, sl])
        qb, kb, vb = q.astype(BF16), k.astype(BF16), v.astype(BF16)
        q32, k32 = q.astype(F32), k.astype(F32)
        C = c_s[h]
        n = n_s[h:h + 1, :]
        m_prev = m_s[h:h + 1, 0:1]
        li = li_all[h:h + 1, :]
        b_row = b_all[h:h + 1, :]
        r_row = li - b_row
        b_col = to_col(b_row)
        r_col = to_col(r_row)
        dmat = jnp.where(causal, b_col + r_row, NEG)
        inter = b_col + m_prev
        mt = jnp.maximum(jnp.max(dmat, axis=1, keepdims=True), inter)
        w = jnp.exp(dmat - mt) * (_nt(qb, kb) * scale)
        a = jnp.exp(inter - mt)
        num = a * _nt(qb, C.astype(BF16)) + jnp.dot(w.astype(BF16), vb, preferred_element_type=F32)
        den = a * jnp.sum(q32 * n, axis=1, keepdims=True) + jnp.sum(w, axis=1, keepdims=True)
        hh = num / jnp.maximum(jnp.abs(den), jnp.exp(-mt))

        bL = b_row[:, cs - 1:cs]
        m_new = jnp.maximum(bL + m_prev, jnp.max(bL + r_row, axis=1, keepdims=True))
        decay = jnp.exp(bL + m_prev - m_new)
        wg_col = jnp.exp(bL + r_col - m_new)
        kw = k32 * wg_col
        c_s[h] = decay * C + scale * _tn(vb, kw.astype(BF16))
        n_s[h:h + 1, :] = decay * n + scale * jnp.sum(kw, axis=0, keepdims=True)
        m_s[h:h + 1, :] = jnp.broadcast_to(m_new, (1, DM))

        mu = jnp.mean(hh, axis=-1, keepdims=True)
        hc = hh - mu
        var = jnp.mean(hc * hc, axis=-1, keepdims=True)
        hn = hc * lax.rsqrt(var + LN_EPS) * mw_ref[:, sl]
        og = _sigmoid(pad_rows(om_ref[:, sl]) + bo_ref[:, sl])
        o_ref[:, sl] = (og * hn)[:rows].astype(o_ref.dtype)

    @pl.when(c == nc - 1)
    def _():
        co_ref[0] = c_s[...]
        no_ref[0] = n_s[...]
        mo_ref[0] = m_s[...]


def _mlstm(gates, gate_bias, pb, om32, b_o, mhn_w, state, *, nseq, cs, rows, valid, H, DM, out_dtype):
    WM = H * DM
    M = pb.shape[0]
    nc = gates.shape[2] // cs
    zero_init = state is None
    blk = lambda col: pl.BlockSpec((rows, WM), lambda b, c: (b * nc + c, col))
    vec = pl.BlockSpec((1, WM), lambda b, c: (0, 0))
    st_c = pl.BlockSpec((1, H, DM, DM), lambda b, c: (b, 0, 0, 0))
    st_v = pl.BlockSpec((1, H, DM), lambda b, c: (b, 0, 0))
    in_specs = [pl.BlockSpec((1, 2 * H, cs), lambda b, c: (b, 0, c)),
                pl.BlockSpec((2 * H, 1), lambda b, c: (0, 0)),
                blk(3), blk(4), blk(5), blk(0), vec, vec]
    args = [gates, gate_bias, pb, pb, pb, om32, b_o.reshape(1, WM), mhn_w.reshape(1, WM)]
    if not zero_init:
        in_specs += [st_c, st_v, st_v]
        args += list(state)
    return pl.pallas_call(
        functools.partial(_mlstm_kernel, cs=cs, rows=rows, valid=valid, zero_init=zero_init, H=H, DM=DM, nc=nc),
        grid=(nseq, nc),
        in_specs=in_specs,
        out_specs=[blk(0), st_c, st_v, st_v],
        out_shape=[jax.ShapeDtypeStruct((M, WM), out_dtype),
                   jax.ShapeDtypeStruct((nseq, H, DM, DM), F32),
                   jax.ShapeDtypeStruct((nseq, H, DM), F32),
                   jax.ShapeDtypeStruct((nseq, H, DM), F32)],
        scratch_shapes=[pltpu.VMEM((H, DM, DM), F32), pltpu.VMEM((H, DM), F32), pltpu.VMEM((H, DM), F32)],
        compiler_params=_cparams(("arbitrary", "arbitrary")),
        name="mlstm",
    )(*args)


def _largest_divisor(n, cap, mult):
    best = None
    for t in range(mult, min(n, cap) + 1, mult):
        if n % t == 0:
            best = t
    assert best is not None, (n, cap, mult)
    return best


def _tiles(L, D, F):
    return dict(
        attn_t=_largest_divisor(L, 256, 128),
        mlstm_cs=_largest_divisor(L, 256, 128),
        inproj_bm=_largest_divisor(L, 512, 8),
        out_bm=_largest_divisor(L, 1024, 8),
        out_bn=_largest_divisor(D, 1024, 128),
        up_bm=_largest_divisor(L, 1024, 8),
        up_bn=_largest_divisor(F, 512, 128),
        down_bm=_largest_divisor(L, 1024, 8),
        down_bn=_largest_divisor(D, 1024, 128),
        down_bk=_largest_divisor(F, 2816, 128),
        ln_tm=_largest_divisor(L, 256, 8),
        pages_per_step=4,
    )


def kernel(x_prompt, x_sample, cache_k, cache_v, page_table, state_C, state_n, state_m, state_conv, rel_bias, w_in, b_i, b_f, b_o, lambda_q1, lambda_k1, lambda_q2, lambda_k2, subln_w, mhn_w, w_out, ln1_g, ln1_b, w_up, conv_w, conv_b, w_down, ln2_g, ln2_b):
    B, L, D = x_prompt.shape
    BS, LS, _ = x_sample.shape
    DEPTH, NPOOL, P, HA, DK = cache_k.shape
    DH = DK // 2
    WA = HA * DK
    HM, DM = state_C.shape[2], state_C.shape[3]
    WM = HM * DM
    F = w_down.shape[1]
    NP = page_table.shape[1]
    NQ = SAMPLE_ROWS
    assert WA == WM and w_in.shape[2] == 3 * WA + 4 * WM + 2 * HM
    assert LS <= NQ and LS >= CONV_W - 1 and P >= _FAR and HA == 8 and HM == 8
    t = _tiles(L, D, F)
    T = t["attn_t"]
    assert T >= _FAR
    G = min(t["pages_per_step"], NP)
    assert NP % G == 0
    alpha = (2 * DEPTH) ** 0.25

    pbias = _prompt_bias(rel_bias, T)
    dbias, nbias = _decode_bias(rel_bias, P, G)
    cache_k2 = cache_k.reshape(DEPTH * NPOOL, P * HA, DK)
    cache_v2 = cache_v.reshape(DEPTH * NPOOL, P * HA, DK)
    zconv = jnp.zeros((B, CONV_W - 1, 2 * F), F32)
    sample_cs = 128

    xp32 = x_prompt.reshape(B * L, D)
    xp16 = xp32.astype(BF16)
    xs32 = jnp.pad(x_sample, ((0, 0), (0, NQ - LS), (0, 0))).reshape(BS * NQ, D)
    xs16 = xs32.astype(BF16)

    outs = {k: [] for k in ("kp", "vp", "ks", "vs", "Cp", "np", "mp", "Cs", "ns", "ms", "cp", "cs")}
    for l in range(DEPTH):
        lam0 = 0.8 - 0.6 * math.exp(-0.3 * l)
        lam_init = jnp.full((1,), lam0, F32)
        lam_vecs = jnp.stack([lambda_q1[l], lambda_k1[l], lambda_q2[l], lambda_k2[l]])
        w_main = w_in[l, :, :3 * WA + 4 * WM].astype(BF16)
        wg = jnp.pad(w_in[l, :, 3 * WA + 4 * WM:], ((0, 0), (0, 128 - 2 * HM))).astype(BF16)
        gate_bias = jnp.concatenate([b_i[l], b_f[l]]).reshape(2 * HM, 1)
        wo16 = w_out[l].astype(BF16)
        wu16 = w_up[l].astype(BF16)
        wd16 = w_down[l].astype(BF16)
        page_idx = (page_table + l * NPOOL).reshape(-1).astype(jnp.int32)

        pb, k32, v32, om32, gates = _inproj(xp16, w_main, wg, bm=t["inproj_bm"], rows_per_seq=L, pb_dtype=BF16,
                                            H=HA, NG=2 * HM)
        oa = _prompt_attention(pb, rel_bias, pbias, lam_vecs, subln_w[l], lam_init, B=B, L=L, H=HA, DH=DH, T=T)
        om, Cn, nn, mn = _mlstm(gates, gate_bias, pb, om32, b_o[l], mhn_w[l], None, nseq=B, cs=t["mlstm_cs"],
                                rows=t["mlstm_cs"], valid=t["mlstm_cs"], H=HM, DM=DM, out_dtype=BF16)
        mix = _mm_pair(oa, om, wo16, bm=t["out_bm"], bn=t["out_bn"])
        h32, h16 = _ln_res(xp32, mix, ln1_g[l], ln1_b[l], alpha=alpha, tm=t["ln_tm"])
        act, ta, tg = _upconv(h16, wu16, conv_w[l], conv_b[l], zconv, bm=t["up_bm"], bn=t["up_bn"],
                              rows_per_seq=L, act_dtype=BF16)
        ff = _mm_k(act, wd16, bm=t["down_bm"], bn=t["down_bn"], bk=t["down_bk"])
        xp32, xp16 = _ln_res(h32, ff, ln2_g[l], ln2_b[l], alpha=alpha, tm=t["ln_tm"])
        outs["kp"].append(k32.reshape(B, L, HA, DK))
        outs["vp"].append(v32.reshape(B, L, HA, DK))
        outs["Cp"].append(Cn)
        outs["np"].append(nn)
        outs["mp"].append(mn[:, :, 0])
        outs["cp"].append(jnp.concatenate([ta[:, 8 - (CONV_W - 1):], tg[:, 8 - (CONV_W - 1):]], axis=-1))

        MS = BS * NQ
        pb, k32, v32, om32, gates = _inproj(xs16, w_main, wg, bm=MS, rows_per_seq=MS, pb_dtype=F32,
                                            H=HA, NG=2 * HM)
        oa = _decode_attention(pb, cache_k2, cache_v2, page_idx, dbias, nbias, lam_vecs, subln_w[l], lam_init,
                               BS=BS, NP=NP, G=G, P=P, H=HA, DH=DH)
        gs = gates.reshape(2 * HM, BS, NQ).transpose(1, 0, 2)
        gs = jnp.pad(gs, ((0, 0), (0, 0), (0, sample_cs - NQ)))
        state = (state_C[l], state_n[l], jnp.broadcast_to(state_m[l][:, :, None], (BS, HM, DM)))
        om, Cn, nn, mn = _mlstm(gs, gate_bias, pb, om32, b_o[l], mhn_w[l], state, nseq=BS, cs=sample_cs,
                                rows=NQ, valid=LS, H=HM, DM=DM, out_dtype=F32)
        mix = _mm_pair(oa, om, wo16, bm=MS, bn=t["out_bn"])
        h32, h16 = _ln_res(xs32, mix, ln1_g[l], ln1_b[l], alpha=alpha, tm=MS)
        act, ta, tg = _upconv(h16, wu16, conv_w[l], conv_b[l], state_conv[l], bm=NQ, bn=t["up_bn"],
                              rows_per_seq=NQ, act_dtype=F32)
        ff = _mm_k(act, wd16, bm=MS, bn=t["down_bn"], bk=t["down_bk"])
        xs32, xs16 = _ln_res(h32, ff, ln2_g[l], ln2_b[l], alpha=alpha, tm=MS)
        outs["ks"].append(k32.reshape(BS, NQ, HA, DK)[:, :LS])
        outs["vs"].append(v32.reshape(BS, NQ, HA, DK)[:, :LS])
        outs["Cs"].append(Cn)
        outs["ns"].append(nn)
        outs["ms"].append(mn[:, :, 0])
        outs["cs"].append(jnp.concatenate([ta[:, LS - (CONV_W - 1):LS], tg[:, LS - (CONV_W - 1):LS]], axis=-1))

    st = lambda k: jnp.stack(outs[k])
    return (xp32.reshape(B, L, D), xs32.reshape(BS, NQ, D)[:, :LS],
            st("kp"), st("vp"), st("ks"), st("vs"),
            st("Cp"), st("np"), st("mp"), st("Cs"), st("ns"), st("ms"),
            st("cp"), st("cs"))
```

```python
import functools
import math

import numpy as np
import jax
import jax.numpy as jnp
from jax import lax
from jax.experimental import pallas as pl
from jax.experimental.pallas import tpu as pltpu

F32 = jnp.float32
BF16 = jnp.bfloat16

LN_EPS = 1e-5
N_BUCKETS = 32
MAX_DISTANCE = 128
CONV_W = 3
NEG = -1e30
SAMPLE_ROWS = 8
V7X_VMEM_LIMIT = 56 * 1024 * 1024


def _bucket_thresholds():
    max_exact = N_BUCKETS // 2
    d = np.arange(0, 4 * MAX_DISTANCE)
    large = max_exact + (np.log(np.maximum(d, 1) / max_exact) / math.log(MAX_DISTANCE / max_exact)
                         * (N_BUCKETS - max_exact)).astype(np.int64)
    bucket = np.where(d < max_exact, d, np.minimum(large, N_BUCKETS - 1))
    return tuple(int(d[bucket >= k].min()) for k in range(1, N_BUCKETS))


_THR = _bucket_thresholds()
_FAR = _THR[-1]


def _cparams(sem):
    return pltpu.CompilerParams(dimension_semantics=sem, vmem_limit_bytes=V7X_VMEM_LIMIT)


def _nt(a, b):
    return lax.dot_general(a, b, (((1,), (1,)), ((), ())), preferred_element_type=F32)


def _tn(a, b):
    return lax.dot_general(a, b, (((0,), (0,)), ((), ())), preferred_element_type=F32)


def _sigmoid(x):
    return 1.0 / (1.0 + jnp.exp(-x))


def _pbias_kernel(rb_ref, o_ref, *, T):
    h = pl.program_id(0)
    r = lax.broadcasted_iota(jnp.int32, (T, T), 0)
    c = lax.broadcasted_iota(jnp.int32, (T, T), 1)
    for t, off in enumerate((0, T)):
        d = r - c + off
        bias = jnp.full((T, T), rb_ref[0, h], F32)
        for k in range(1, N_BUCKETS):
            bias = jnp.where(d >= _THR[k - 1], rb_ref[k, h], bias)
        if off == 0:
            bias = jnp.where(d >= 0, bias, NEG)
        o_ref[0, t] = bias


def _prompt_bias(rel_bias, T):
    H = rel_bias.shape[1]
    return pl.pallas_call(
        functools.partial(_pbias_kernel, T=T),
        grid=(H,),
        in_specs=[pl.BlockSpec(memory_space=pltpu.SMEM)],
        out_specs=pl.BlockSpec((1, 2, T, T), lambda h: (h, 0, 0, 0)),
        out_shape=jax.ShapeDtypeStruct((H, 2, T, T), F32),
        compiler_params=_cparams(("arbitrary",)),
        name="prompt_bias",
    )(rel_bias)


def _dbias_kernel(rb_ref, d_ref, n_ref, *, P, G):
    h = pl.program_id(0)
    NQ = SAMPLE_ROWS
    R = 2 * NQ

    def build(d):
        bias = jnp.full(d.shape, rb_ref[0, h], F32)
        for k in range(1, N_BUCKETS):
            bias = jnp.where(d >= _THR[k - 1], rb_ref[k, h], bias)
        return bias

    r = lax.broadcasted_iota(jnp.int32, (R, P), 0)
    c = lax.broadcasted_iota(jnp.int32, (R, P), 1)
    qi = r % NQ
    far = jnp.full((R, P), rb_ref[N_BUCKETS - 1, h], F32)
    last = build(P + qi - c)
    dn = qi - c
    new = jnp.where(dn >= 0, build(jnp.maximum(dn, 0)), NEG)
    d_ref[0] = jnp.concatenate([far] * G, axis=1)
    d_ref[1] = jnp.concatenate([far] * (G - 1) + [last], axis=1)
    n_ref[...] = new


def _decode_bias(rel_bias, P, G):
    H = rel_bias.shape[1]
    R = 2 * SAMPLE_ROWS
    return pl.pallas_call(
        functools.partial(_dbias_kernel, P=P, G=G),
        grid=(H,),
        in_specs=[pl.BlockSpec(memory_space=pltpu.SMEM)],
        out_specs=[pl.BlockSpec((2, R, G * P), lambda h: (0, h, 0)),
                   pl.BlockSpec((R, P), lambda h: (h, 0))],
        out_shape=(jax.ShapeDtypeStruct((2, H * R, G * P), F32), jax.ShapeDtypeStruct((H * R, P), F32)),
        compiler_params=_cparams(("arbitrary",)),
        name="decode_bias",
    )(rel_bias)


def _inproj_kernel(x_ref, w_ref, wg_ref, pb_ref, k_ref, v_ref, om_ref, g_ref, *, H, NG):
    j = pl.program_id(1)
    x = x_ref[...].astype(BF16)
    acc = jnp.dot(x, w_ref[...], preferred_element_type=F32)
    bm, W = acc.shape
    DK = W // H

    def store_heads(ref):
        for h in range(H):
            ref[pl.ds(h, bm, stride=H), :] = acc[:, h * DK:(h + 1) * DK]

    @pl.when(j <= 5)
    def _():
        pb_ref[...] = acc.astype(pb_ref.dtype)

    @pl.when(j == 1)
    def _():
        store_heads(k_ref)

    @pl.when(j == 2)
    def _():
        store_heads(v_ref)

    @pl.when(j == 6)
    def _():
        om_ref[...] = acc

    @pl.when(j == 0)
    def _():
        gates = jnp.dot(x, wg_ref[...], preferred_element_type=F32)
        if bm % 128:
            gates = jnp.concatenate([gates, jnp.zeros((128 - bm % 128, gates.shape[1]), F32)], axis=0)
        g_ref[0] = gates.T[:NG, :bm]


def _inproj(x, w, layer, wg, *, bm, rows_per_seq, pb_dtype, W, H, NG):
    M, D = x.shape
    DK = W // H
    tpb = rows_per_seq // bm
    grid = (M // bm, 7)
    return pl.pallas_call(
        functools.partial(_inproj_kernel, H=H, NG=NG),
        grid=grid,
        in_specs=[pl.BlockSpec((bm, D), lambda i, j: (i, 0)),
                  pl.BlockSpec((None, D, W), lambda i, j: (layer, 0, j)),
                  pl.BlockSpec(wg.shape, lambda i, j: (0, 0))],
        out_specs=[pl.BlockSpec((bm, W), lambda i, j: (i, jnp.minimum(j, 5))),
                   pl.BlockSpec((bm * H, DK), lambda i, j: (i, 0)),
                   pl.BlockSpec((bm * H, DK), lambda i, j: (i, 0)),
                   pl.BlockSpec((bm, W), lambda i, j: (i, 0)),
                   pl.BlockSpec((1, NG, bm), lambda i, j: (i // tpb, 0, i % tpb))],
        out_shape=[jax.ShapeDtypeStruct((M, 6 * W), pb_dtype),
                   jax.ShapeDtypeStruct((M * H, DK), F32),
                   jax.ShapeDtypeStruct((M * H, DK), F32),
                   jax.ShapeDtypeStruct((M, W), F32),
                   jax.ShapeDtypeStruct((M // rows_per_seq, NG, rows_per_seq), F32)],
        compiler_params=_cparams(("arbitrary", "arbitrary")),
        name="in_proj",
    )(x, w, wg)


def _deepnorm(res, y, g_ref, b_ref, o32_ref, o16_ref, alpha):
    z = alpha * res + y
    mu = jnp.mean(z, axis=-1, keepdims=True)
    zc = z - mu
    var = jnp.mean(zc * zc, axis=-1, keepdims=True)
    o = zc * lax.rsqrt(var + LN_EPS) * g_ref[...] + b_ref[...]
    o32_ref[...] = o
    o16_ref[...] = o.astype(BF16)


def _outproj_ln_kernel(x1_ref, x2_ref, w1_ref, w2_ref, r_ref, g_ref, b_ref, o32_ref, o16_ref, *, alpha):
    mix = (jnp.dot(x1_ref[...].astype(BF16), w1_ref[...], preferred_element_type=F32)
           + jnp.dot(x2_ref[...].astype(BF16), w2_ref[...], preferred_element_type=F32))
    _deepnorm(r_ref[...], mix, g_ref, b_ref, o32_ref, o16_ref, alpha)


def _outproj_ln(x1, x2, w, layer, res, g, b, *, alpha, bm):
    M, K1 = x1.shape
    N = w.shape[2]
    row = pl.BlockSpec((bm, N), lambda i: (i, 0))
    vec = pl.BlockSpec((1, N), lambda i: (0, 0))
    return pl.pallas_call(
        functools.partial(_outproj_ln_kernel, alpha=alpha),
        grid=(M // bm,),
        in_specs=[pl.BlockSpec((bm, K1), lambda i: (i, 0)),
                  pl.BlockSpec((bm, K1), lambda i: (i, 0)),
                  pl.BlockSpec((None, K1, N), lambda i: (layer, 0, 0)),
                  pl.BlockSpec((None, K1, N), lambda i: (layer, 1, 0)),
                  row, vec, vec],
        out_specs=[row, row],
        out_shape=[jax.ShapeDtypeStruct((M, N), F32), jax.ShapeDtypeStruct((M, N), BF16)],
        compiler_params=_cparams(("arbitrary",)),
        name="out_proj_ln",
    )(x1, x2, w, w, res, g.reshape(1, N), b.reshape(1, N))


def _down_ln_kernel(x_ref, w_ref, r_ref, g_ref, b_ref, o32_ref, o16_ref, acc_ref, *, alpha, nk):
    k = pl.program_id(1)
    d = jnp.dot(x_ref[...].astype(BF16), w_ref[...], preferred_element_type=F32)

    @pl.when(k == 0)
    def _():
        acc_ref[...] = d

    @pl.when(k > 0)
    def _():
        acc_ref[...] += d

    @pl.when(k == nk - 1)
    def _():
        _deepnorm(r_ref[...], acc_ref[...], g_ref, b_ref, o32_ref, o16_ref, alpha)


def _down_ln(x, w, layer, res, g, b, *, alpha, bm, bk):
    M, K = x.shape
    N = w.shape[2]
    nk = K // bk
    row = pl.BlockSpec((bm, N), lambda i, k: (i, 0))
    vec = pl.BlockSpec((1, N), lambda i, k: (0, 0))
    return pl.pallas_call(
        functools.partial(_down_ln_kernel, alpha=alpha, nk=nk),
        grid=(M // bm, nk),
        in_specs=[pl.BlockSpec((bm, bk), lambda i, k: (i, k)),
                  pl.BlockSpec((None, bk, N), lambda i, k: (layer, k, 0)),
                  row, vec, vec],
        out_specs=[row, row],
        out_shape=[jax.ShapeDtypeStruct((M, N), F32), jax.ShapeDtypeStruct((M, N), BF16)],
        scratch_shapes=[pltpu.VMEM((bm, N), F32)],
        compiler_params=_cparams(("arbitrary", "arbitrary")),
        name="down_proj_ln",
    )(x, w, res, g.reshape(1, N), b.reshape(1, N))


def _upconv_kernel(x_ref, wa_ref, wg_ref, cwa_ref, cwg_ref, cba_ref, cbg_ref, c0a_ref, c0g_ref,
                   act_ref, ta_ref, tg_ref, ha_ref, hg_ref, *, bm, nmt):
    i = pl.program_id(2)
    x = x_ref[...].astype(BF16)
    rid = lax.broadcasted_iota(jnp.int32, (8, act_ref.shape[1]), 0)
    conv = []
    for w_ref, cw_ref, cb_ref, c0_ref, t_ref, h_ref in ((wa_ref, cwa_ref, cba_ref, c0a_ref, ta_ref, ha_ref),
                                                        (wg_ref, cwg_ref, cbg_ref, c0g_ref, tg_ref, hg_ref)):
        u = jnp.dot(x, w_ref[...], preferred_element_type=F32)
        c0 = c0_ref[0]
        prev = h_ref[...]
        first = i == 0
        p2 = jnp.where(first, c0[0:1], prev[6:7])
        p1 = jnp.where(first, c0[1:2], prev[7:8])
        u1 = pltpu.roll(u, 1, axis=0)
        u2 = pltpu.roll(u, 2, axis=0)
        h1 = jnp.where(rid == 0, p1, u1[:8])
        h2 = jnp.where(rid == 0, p2, jnp.where(rid == 1, p1, u2[:8]))
        if bm > 8:
            u1 = jnp.concatenate([h1, u1[8:]], axis=0)
            u2 = jnp.concatenate([h2, u2[8:]], axis=0)
        else:
            u1, u2 = h1, h2
        cw = cw_ref[...]
        conv.append(cb_ref[...] + cw[0:1] * u2 + cw[1:2] * u1 + cw[2:3] * u)
        h_ref[...] = u[bm - 8:]

        @pl.when(i == nmt - 1)
        def _():
            t_ref[0] = u[bm - 8:]

    a_br, g_br = conv
    act_ref[...] = (g_br * _sigmoid(g_br) * a_br).astype(act_ref.dtype)


def _upconv(x, w_up, layer, conv_w, conv_b, conv0, *, bm, bn, rows_per_seq, act_dtype):
    M, D = x.shape
    F = w_up.shape[2] // 2
    NT = F // bn
    nb = M // rows_per_seq
    nmt = rows_per_seq // bm
    cb = conv_b.reshape(1, 2 * F)
    half = lambda off: (lambda j, b, i: (0, off + j))
    wsel = lambda off: (lambda j, b, i: (layer, 0, off + j))
    st = lambda off: (lambda j, b, i: (b, 0, off + j))
    return pl.pallas_call(
        functools.partial(_upconv_kernel, bm=bm, nmt=nmt),
        grid=(NT, nb, nmt),
        in_specs=[pl.BlockSpec((bm, D), lambda j, b, i: (b * nmt + i, 0)),
                  pl.BlockSpec((None, D, bn), wsel(0)), pl.BlockSpec((None, D, bn), wsel(NT)),
                  pl.BlockSpec((CONV_W, bn), half(0)), pl.BlockSpec((CONV_W, bn), half(NT)),
                  pl.BlockSpec((1, bn), half(0)), pl.BlockSpec((1, bn), half(NT)),
                  pl.BlockSpec((1, CONV_W - 1, bn), st(0)), pl.BlockSpec((1, CONV_W - 1, bn), st(NT))],
        out_specs=[pl.BlockSpec((bm, bn), lambda j, b, i: (b * nmt + i, j)),
                   pl.BlockSpec((1, 8, bn), st(0)), pl.BlockSpec((1, 8, bn), st(0))],
        out_shape=[jax.ShapeDtypeStruct((M, F), act_dtype),
                   jax.ShapeDtypeStruct((nb, 8, F), F32), jax.ShapeDtypeStruct((nb, 8, F), F32)],
        scratch_shapes=[pltpu.VMEM((8, bn), F32), pltpu.VMEM((8, bn), F32)],
        compiler_params=_cparams(("arbitrary", "arbitrary", "arbitrary")),
        name="up_conv_gate",
    )(x, w_up, w_up, conv_w, conv_w, cb, cb, conv0, conv0)


def _lambda(lv_ref, lam_init):
    lv = lv_ref[...]
    s1 = jnp.sum(lv[0:1] * lv[1:2], axis=1, keepdims=True)
    s2 = jnp.sum(lv[2:3] * lv[3:4], axis=1, keepdims=True)
    return jnp.exp(s1) - jnp.exp(s2) + lam_init


def _pattn_kernel(li_ref, rb_ref, q_ref, k_ref, v_ref, pbias_ref, lv_ref, sw_ref, o_ref,
                  m_ref, l_ref, acc_ref, *, T, DH):
    h = pl.program_id(1)
    i = pl.program_id(2)
    lane = lax.broadcasted_iota(jnp.int32, q_ref.shape, 1)
    q = q_ref[...].astype(F32) * (DH ** -0.5)
    qs = [jnp.where(lane < DH, q, 0.0).astype(BF16), jnp.where(lane >= DH, q, 0.0).astype(BF16)]
    m_ref[...] = jnp.full(m_ref.shape, NEG, F32)
    l_ref[...] = jnp.zeros(l_ref.shape, F32)
    acc_ref[...] = jnp.zeros(acc_ref.shape, F32)
    reps = T // m_ref.shape[2]

    def tile(j, bias):
        start = pl.multiple_of(j * T, T)
        kt = k_ref[pl.ds(start, T), :]
        vt = v_ref[pl.ds(start, T), :]
        for mp in range(2):
            s = _nt(qs[mp], kt) + bias
            m_old = m_ref[mp]
            m_new = jnp.maximum(m_old, jnp.max(s, axis=1, keepdims=True))
            alpha = jnp.exp(m_old - m_new)
            p = jnp.exp(s - jnp.concatenate([m_new] * reps, axis=1))
            l_ref[mp] = alpha * l_ref[mp] + jnp.sum(p, axis=1, keepdims=True)
            acc_ref[mp] = alpha * acc_ref[mp] + jnp.dot(p.astype(BF16), vt, preferred_element_type=F32)
            m_ref[mp] = m_new

    far_bias = rb_ref[N_BUCKETS - 1, h]

    def far_body(j, carry):
        tile(j, far_bias)
        return carry

    lax.fori_loop(0, jnp.maximum(i - 1, 0), far_body, 0)

    @pl.when(i >= 1)
    def _():
        tile(i - 1, pbias_ref[0, 1])

    tile(i, pbias_ref[0, 0])

    lam_init = li_ref[0]
    lam = _lambda(lv_ref, lam_init)
    o = acc_ref[0] / l_ref[0] - lam * (acc_ref[1] / l_ref[1])
    o = o * lax.rsqrt(jnp.mean(o * o, axis=-1, keepdims=True) + LN_EPS) * sw_ref[...] * (1.0 - lam_init)
    o_ref[...] = o.astype(o_ref.dtype)


def _prompt_attention(pb, rel_bias, pbias, lam_vecs, subln_w, lam_init, *, B, L, H, DH, T):
    M = pb.shape[0]
    DV = 2 * DH
    nq = L // T
    return pl.pallas_call(
        functools.partial(_pattn_kernel, T=T, DH=DH),
        grid=(B, H, nq),
        in_specs=[pl.BlockSpec(memory_space=pltpu.SMEM),
                  pl.BlockSpec(memory_space=pltpu.SMEM),
                  pl.BlockSpec((T, DV), lambda b, h, i: (b * nq + i, h)),
                  pl.BlockSpec((L, DV), lambda b, h, i: (b, H + h)),
                  pl.BlockSpec((L, DV), lambda b, h, i: (b, 2 * H + h)),
                  pl.BlockSpec((1, 2, T, T), lambda b, h, i: (h, 0, 0, 0)),
                  pl.BlockSpec((4, DH), lambda b, h, i: (0, 0)),
                  pl.BlockSpec((1, DV), lambda b, h, i: (0, 0))],
        out_specs=pl.BlockSpec((T, DV), lambda b, h, i: (b * nq + i, h)),
        out_shape=jax.ShapeDtypeStruct((M, H * DV), BF16),
        scratch_shapes=[pltpu.VMEM((2, T, DV), F32), pltpu.VMEM((2, T, DV), F32), pltpu.VMEM((2, T, DV), F32)],
        compiler_params=_cparams(("arbitrary", "arbitrary", "arbitrary")),
        name="prompt_attention",
    )(lam_init, rel_bias, pb, pb, pb, pbias, lam_vecs, subln_w.reshape(1, DV))


def _dattn_kernel(pt_ref, li_ref, q_ref, kn_ref, vn_ref, db_ref, nb_ref, lv_ref, sw_ref, *rest,
                  G, P, H, DH, nsteps):
    del pt_ref
    k_refs = rest[:G]
    v_refs = rest[G:2 * G]
    o_ref, qb_ref, m_ref, l_ref, acc_ref = rest[2 * G:]
    NQ = SAMPLE_ROWS
    DV = 2 * DH
    WA = H * DV
    RH = 2 * NQ
    s_id = pl.program_id(1)

    @pl.when(s_id == 0)
    def _():
        q = q_ref[...] * (DH ** -0.5)
        lane = lax.broadcasted_iota(jnp.int32, (NQ, WA), 1)
        blocks = []
        for h in range(H):
            for mp in range(2):
                lo = h * DV + mp * DH
                blocks.append(jnp.where((lane >= lo) & (lane < lo + DH), q, 0.0))
        qb_ref[...] = jnp.concatenate(blocks, axis=0).astype(BF16)
        m_ref[...] = jnp.full(m_ref.shape, NEG, F32)
        l_ref[...] = jnp.zeros(l_ref.shape, F32)
        acc_ref[...] = jnp.zeros(acc_ref.shape, F32)

    def update(kc, v_heads, bias):
        s = _nt(qb_ref[...], kc) + bias
        m_old = m_ref[...]
        m_new = jnp.maximum(m_old, jnp.max(s, axis=1, keepdims=True))
        alpha = jnp.exp(m_old - m_new)
        p = jnp.exp(s - jnp.concatenate([m_new] * (s.shape[1] // DV), axis=1))
        l_ref[...] = alpha * l_ref[...] + jnp.sum(p, axis=1, keepdims=True)
        pb = p.astype(BF16)
        pv = jnp.concatenate([jnp.dot(pb[h * RH:(h + 1) * RH], v_heads[h], preferred_element_type=F32)
                              for h in range(H)], axis=0)
        acc_ref[...] = alpha * acc_ref[...] + pv
        m_ref[...] = m_new

    def head_rows(refs, h):
        return jnp.concatenate([r[pl.ds(h, P, stride=H), :] for r in refs], axis=0).astype(BF16)

    is_last = (s_id == nsteps - 1).astype(jnp.int32)
    kc = jnp.concatenate([head_rows(k_refs, h) for h in range(H)], axis=1)
    update(kc, [head_rows(v_refs, h) for h in range(H)], db_ref[is_last])

    @pl.when(s_id == nsteps - 1)
    def _():
        lam_init = li_ref[0]
        lam = _lambda(lv_ref, lam_init)
        pad = jnp.zeros((P - NQ, WA), F32)
        kn = jnp.concatenate([kn_ref[...], pad], axis=0).astype(BF16)
        vn = jnp.concatenate([vn_ref[...], pad], axis=0).astype(BF16)
        update(kn, [vn[:, h * DV:(h + 1) * DV] for h in range(H)], nb_ref[...])
        a = acc_ref[...] / l_ref[...]
        for h in range(H):
            o = a[h * RH:h * RH + NQ] - lam * a[h * RH + NQ:(h + 1) * RH]
            o = o * lax.rsqrt(jnp.mean(o * o, axis=-1, keepdims=True) + LN_EPS) * sw_ref[...] * (1.0 - lam_init)
            o_ref[:, h * DV:(h + 1) * DV] = o


def _decode_attention(pb_s, cache_k, cache_v, page_idx, dbias, nbias, lam_vecs, subln_w, lam_init,
                      *, BS, NP, G, P, H, DH):
    NQ = SAMPLE_ROWS
    DV = 2 * DH
    WA = H * DV
    R = 2 * NQ * H
    nsteps = NP // G
    row = lambda col: pl.BlockSpec((NQ, WA), lambda b, s, pt: (b, col))
    full = lambda shape: pl.BlockSpec(shape, lambda b, s, pt: (0,) * len(shape))
    page = lambda g: pl.BlockSpec((None, P * H, DV), lambda b, s, pt: (pt[b * NP + s * G + g], 0, 0))
    grid_spec = pltpu.PrefetchScalarGridSpec(
        num_scalar_prefetch=1,
        grid=(BS, nsteps),
        in_specs=[pl.BlockSpec(memory_space=pltpu.SMEM),
                  row(0), row(1), row(2),
                  full((2, R, G * P)), full((R, P)), full((4, DH)), full((1, DV))]
                 + [page(g) for g in range(G)] + [page(g) for g in range(G)],
        out_specs=pl.BlockSpec((NQ, WA), lambda b, s, pt: (b, 0)),
        scratch_shapes=[pltpu.VMEM((R, WA), BF16), pltpu.VMEM((R, DV), F32), pltpu.VMEM((R, DV), F32),
                        pltpu.VMEM((R, DV), F32)])
    return pl.pallas_call(
        functools.partial(_dattn_kernel, G=G, P=P, H=H, DH=DH, nsteps=nsteps),
        grid_spec=grid_spec,
        out_shape=jax.ShapeDtypeStruct((BS * NQ, WA), F32),
        compiler_params=_cparams(("arbitrary", "arbitrary")),
        name="decode_attention",
    )(page_idx, lam_init, pb_s, pb_s, pb_s, dbias, nbias, lam_vecs, subln_w.reshape(1, DV),
      *([cache_k] * G), *([cache_v] * G))


def _log_sigmoid(x):
    return jnp.minimum(x, 0.0) - jnp.log1p(jnp.exp(-jnp.abs(x)))


def _mlstm_kernel(*refs, cs, rows, valid, zero_init, H, DM, nc):
    if zero_init:
        g_ref, gb_ref, q_ref, k_ref, v_ref, om_ref, bo_ref, mw_ref = refs[:8]
        rest = refs[8:]
    else:
        g_ref, gb_ref, q_ref, k_ref, v_ref, om_ref, bo_ref, mw_ref, c0_ref, n0_ref, m0_ref = refs[:11]
        rest = refs[11:]
    o_ref, co_ref, no_ref, mo_ref, c_s, n_s, m_s = rest
    c = pl.program_id(1)
    scale = DM ** -0.5

    @pl.when(c == 0)
    def _():
        if zero_init:
            c_s[...] = jnp.zeros(c_s.shape, F32)
            n_s[...] = jnp.zeros(n_s.shape, F32)
            m_s[...] = jnp.zeros(m_s.shape, F32)
        else:
            c_s[...] = c0_ref[0]
            n_s[...] = n0_ref[0]
            m_s[...] = m0_ref[0]

    g = g_ref[0] + gb_ref[...]
    li_all = g[:H]
    lf_all = _log_sigmoid(g[H:])
    if valid < cs:
        col = lax.broadcasted_iota(jnp.int32, (H, cs), 1)
        li_all = jnp.where(col < valid, li_all, NEG)
        lf_all = jnp.where(col < valid, lf_all, 0.0)
    rr = lax.broadcasted_iota(jnp.int32, (cs, cs), 0)
    cc = lax.broadcasted_iota(jnp.int32, (cs, cs), 1)
    causal = cc <= rr
    eye = cc == rr
    upper = (rr <= cc).astype(F32)
    b_all = jnp.dot(lf_all, upper, preferred_element_type=F32, precision=lax.Precision.HIGHEST)

    def pad_rows(x):
        if rows == cs:
            return x
        return jnp.concatenate([x, jnp.zeros((cs - rows, x.shape[1]), x.dtype)], axis=0)

    def to_col(row):
        return jnp.sum(jnp.where(eye, row, 0.0), axis=1, keepdims=True)

    for h in range(H):
        sl = slice(h * DM, (h + 1) * DM)
        q = pad_rows(q_ref[:, sl])
        k = pad_rows(k_ref[:, sl])
        v = pad_rows(v_ref[:, sl])
        qb, kb, vb = q.astype(BF16), k.astype(BF16), v.astype(BF16)
        q32, k32 = q.astype(F32), k.astype(F32)
        C = c_s[h]
        n = n_s[h:h + 1, :]
        m_prev = m_s[h:h + 1, 0:1]
        li = li_all[h:h + 1, :]
        b_row = b_all[h:h + 1, :]
        r_row = li - b_row
        b_col = to_col(b_row)
        r_col = to_col(r_row)
        dmat = jnp.where(causal, b_col + r_row, NEG)
        inter = b_col + m_prev
        mt = jnp.maximum(jnp.max(dmat, axis=1, keepdims=True), inter)
        w = jnp.exp(dmat - mt) * (_nt(qb, kb) * scale)
        a = jnp.exp(inter - mt)
        num = a * _nt(qb, C.astype(BF16)) + jnp.dot(w.astype(BF16), vb, preferred_element_type=F32)
        den = a * jnp.sum(q32 * n, axis=1, keepdims=True) + jnp.sum(w, axis=1, keepdims=True)
        hh = num / jnp.maximum(jnp.abs(den), jnp.exp(-mt))

        bL = b_row[:, cs - 1:cs]
        m_new = jnp.maximum(bL + m_prev, jnp.max(bL + r_row, axis=1, keepdims=True))
        decay = jnp.exp(bL + m_prev - m_new)
        wg_col = jnp.exp(bL + r_col - m_new)
        kw = k32 * wg_col
        c_s[h] = decay * C + scale * _tn(vb, kw.astype(BF16))
        n_s[h:h + 1, :] = decay * n + scale * jnp.sum(kw, axis=0, keepdims=True)
        m_s[h:h + 1, :] = jnp.broadcast_to(m_new, (1, DM))

        mu = jnp.mean(hh, axis=-1, keepdims=True)
        hc = hh - mu
        var = jnp.mean(hc * hc, axis=-1, keepdims=True)
        hn = hc * lax.rsqrt(var + LN_EPS) * mw_ref[:, sl]
        og = _sigmoid(pad_rows(om_ref[:, sl]) + bo_ref[:, sl])
        o_ref[:, sl] = (og * hn)[:rows].astype(o_ref.dtype)

    @pl.when(c == nc - 1)
    def _():
        co_ref[0] = c_s[...]
        no_ref[0] = n_s[...]
        mo_ref[0] = m_s[...]


def _mlstm(gates, gate_bias, pb, om32, b_o, mhn_w, state, *, nseq, cs, rows, valid, H, DM, out_dtype):
    WM = H * DM
    M = pb.shape[0]
    nc = gates.shape[2] // cs
    zero_init = state is None
    blk = lambda col: pl.BlockSpec((rows, WM), lambda b, c: (b * nc + c, col))
    vec = pl.BlockSpec((1, WM), lambda b, c: (0, 0))
    st_c = pl.BlockSpec((1, H, DM, DM), lambda b, c: (b, 0, 0, 0))
    st_v = pl.BlockSpec((1, H, DM), lambda b, c: (b, 0, 0))
    in_specs = [pl.BlockSpec((1, 2 * H, cs), lambda b, c: (b, 0, c)),
                pl.BlockSpec((2 * H, 1), lambda b, c: (0, 0)),
                blk(3), blk(4), blk(5), blk(0), vec, vec]
    args = [gates, gate_bias, pb, pb, pb, om32, b_o.reshape(1, WM), mhn_w.reshape(1, WM)]
    if not zero_init:
        in_specs += [st_c, st_v, st_v]
        args += list(state)
    return pl.pallas_call(
        functools.partial(_mlstm_kernel, cs=cs, rows=rows, valid=valid, zero_init=zero_init, H=H, DM=DM, nc=nc),
        grid=(nseq, nc),
        in_specs=in_specs,
        out_specs=[blk(0), st_c, st_v, st_v],
        out_shape=[jax.ShapeDtypeStruct((M, WM), out_dtype),
                   jax.ShapeDtypeStruct((nseq, H, DM, DM), F32),
                   jax.ShapeDtypeStruct((nseq, H, DM), F32),
                   jax.ShapeDtypeStruct((nseq, H, DM), F32)],
        scratch_shapes=[pltpu.VMEM((H, DM, DM), F32), pltpu.VMEM((H, DM), F32), pltpu.VMEM((H, DM), F32)],
        compiler_params=_cparams(("arbitrary", "arbitrary")),
        name="mlstm",
    )(*args)


def _largest_divisor(n, cap, mult):
    best = None
    for t in range(mult, min(n, cap) + 1, mult):
        if n % t == 0:
            best = t
    assert best is not None, (n, cap, mult)
    return best


def _tiles(L, D, F):
    return dict(
        attn_t=_largest_divisor(L, 512, 128),
        mlstm_cs=_largest_divisor(L, 256, 128),
        inproj_bm=_largest_divisor(L, 512, 8),
        out_bm=_largest_divisor(L, 512, 8),
        up_bm=_largest_divisor(L, 1024, 8),
        up_bn=_largest_divisor(F, 512, 128),
        up_bn_sample=_largest_divisor(F, 1408, 128),
        down_bm=_largest_divisor(L, 512, 8),
        down_bk=_largest_divisor(F, 1408, 128),
        pages_per_step=4,
    )


def kernel(x_prompt, x_sample, cache_k, cache_v, page_table, state_C, state_n, state_m, state_conv, rel_bias, w_in, b_i, b_f, b_o, lambda_q1, lambda_k1, lambda_q2, lambda_k2, subln_w, mhn_w, w_out, ln1_g, ln1_b, w_up, conv_w, conv_b, w_down, ln2_g, ln2_b):
    B, L, D = x_prompt.shape
    BS, LS, _ = x_sample.shape
    DEPTH, NPOOL, P, HA, DK = cache_k.shape
    DH = DK // 2
    WA = HA * DK
    HM, DM = state_C.shape[2], state_C.shape[3]
    WM = HM * DM
    F = w_down.shape[1]
    NP = page_table.shape[1]
    NQ = SAMPLE_ROWS
    assert WA == WM and w_in.shape[2] == 3 * WA + 4 * WM + 2 * HM
    assert LS <= NQ and LS >= CONV_W - 1 and P >= _FAR and HA == 8 and HM == 8
    t = _tiles(L, D, F)
    T = t["attn_t"]
    assert T >= _FAR
    G = min(t["pages_per_step"], NP)
    assert NP % G == 0
    alpha = (2 * DEPTH) ** 0.25

    pbias = _prompt_bias(rel_bias, T)
    dbias, nbias = _decode_bias(rel_bias, P, G)
    cache_k2 = cache_k.reshape(DEPTH * NPOOL, P * HA, DK)
    cache_v2 = cache_v.reshape(DEPTH * NPOOL, P * HA, DK)
    zconv = jnp.zeros((B, CONV_W - 1, 2 * F), F32)
    sample_cs = 128

    xp32 = x_prompt.reshape(B * L, D)
    xp16 = xp32.astype(BF16)
    xs32 = jnp.pad(x_sample, ((0, 0), (0, NQ - LS), (0, 0))).reshape(BS * NQ, D)
    xs16 = xs32.astype(BF16)
    wi16 = w_in.astype(BF16)
    wo16 = w_out.astype(BF16)
    wu16 = w_up.astype(BF16)
    wd16 = w_down.astype(BF16)

    outs = {k: [] for k in ("kp", "vp", "ks", "vs", "Cp", "np", "mp", "Cs", "ns", "ms", "cp", "cs")}
    for l in range(DEPTH):
        lam0 = 0.8 - 0.6 * math.exp(-0.3 * l)
        lam_init = jnp.full((1,), lam0, F32)
        lam_vecs = jnp.stack([lambda_q1[l], lambda_k1[l], lambda_q2[l], lambda_k2[l]])
        wg = jnp.pad(w_in[l, :, 3 * WA + 4 * WM:], ((0, 0), (0, 128 - 2 * HM))).astype(BF16)
        gate_bias = jnp.concatenate([b_i[l], b_f[l]]).reshape(2 * HM, 1)
        page_idx = (page_table + l * NPOOL).reshape(-1).astype(jnp.int32)

        pb, k32, v32, om32, gates = _inproj(xp16, wi16, l, wg, bm=t["inproj_bm"], rows_per_seq=L, pb_dtype=BF16,
                                            W=WA, H=HA, NG=2 * HM)
        oa = _prompt_attention(pb, rel_bias, pbias, lam_vecs, subln_w[l], lam_init, B=B, L=L, H=HA, DH=DH, T=T)
        om, Cn, nn, mn = _mlstm(gates, gate_bias, pb, om32, b_o[l], mhn_w[l], None, nseq=B, cs=t["mlstm_cs"],
                                rows=t["mlstm_cs"], valid=t["mlstm_cs"], H=HM, DM=DM, out_dtype=BF16)
        h32, h16 = _outproj_ln(oa, om, wo16, l, xp32, ln1_g[l], ln1_b[l], alpha=alpha, bm=t["out_bm"])
        act, ta, tg = _upconv(h16, wu16, l, conv_w[l], conv_b[l], zconv, bm=t["up_bm"], bn=t["up_bn"],
                              rows_per_seq=L, act_dtype=BF16)
        xp32, xp16 = _down_ln(act, wd16, l, h32, ln2_g[l], ln2_b[l], alpha=alpha, bm=t["down_bm"], bk=t["down_bk"])
        outs["kp"].append(k32.reshape(B, L, HA, DK))
        outs["vp"].append(v32.reshape(B, L, HA, DK))
        outs["Cp"].append(Cn)
        outs["np"].append(nn)
        outs["mp"].append(mn[:, :, 0])
        outs["cp"].append(jnp.concatenate([ta[:, 8 - (CONV_W - 1):], tg[:, 8 - (CONV_W - 1):]], axis=-1))

        MS = BS * NQ
        pb, k32, v32, om32, gates = _inproj(xs16, wi16, l, wg, bm=MS, rows_per_seq=MS, pb_dtype=F32,
                                            W=WA, H=HA, NG=2 * HM)
        oa = _decode_attention(pb, cache_k2, cache_v2, page_idx, dbias, nbias, lam_vecs, subln_w[l], lam_init,
                               BS=BS, NP=NP, G=G, P=P, H=HA, DH=DH)
        gs = gates.reshape(2 * HM, BS, NQ).transpose(1, 0, 2)
        gs = jnp.pad(gs, ((0, 0), (0, 0), (0, sample_cs - NQ)))
        state = (state_C[l], state_n[l], jnp.broadcast_to(state_m[l][:, :, None], (BS, HM, DM)))
        om, Cn, nn, mn = _mlstm(gs, gate_bias, pb, om32, b_o[l], mhn_w[l], state, nseq=BS, cs=sample_cs,
                                rows=NQ, valid=LS, H=HM, DM=DM, out_dtype=F32)
        h32, h16 = _outproj_ln(oa, om, wo16, l, xs32, ln1_g[l], ln1_b[l], alpha=alpha, bm=MS)
        act, ta, tg = _upconv(h16, wu16, l, conv_w[l], conv_b[l], state_conv[l], bm=NQ, bn=t["up_bn_sample"],
                              rows_per_seq=NQ, act_dtype=F32)
        xs32, xs16 = _down_ln(act, wd16, l, h32, ln2_g[l], ln2_b[l], alpha=alpha, bm=MS, bk=t["down_bk"])
        outs["ks"].append(k32.reshape(BS, NQ, HA, DK)[:, :LS])
        outs["vs"].append(v32.reshape(BS, NQ, HA, DK)[:, :LS])
        outs["Cs"].append(Cn)
        outs["ns"].append(nn)
        outs["ms"].append(mn[:, :, 0])
        outs["cs"].append(jnp.concatenate([ta[:, LS - (CONV_W - 1):LS], tg[:, LS - (CONV_W - 1):LS]], axis=-1))

    st = lambda k: jnp.stack(outs[k])
    return (xp32.reshape(B, L, D), xs32.reshape(BS, NQ, D)[:, :LS],
            st("kp"), st("vp"), st("ks"), st("vs"),
            st("Cp"), st("np"), st("mp"), st("Cs"), st("ns"), st("ms"),
            st("cp"), st("cs"))
```

```python
import functools
import math

import numpy as np
import jax
import jax.numpy as jnp
from jax import lax
from jax.experimental import pallas as pl
from jax.experimental.pallas import tpu as pltpu

F32 = jnp.float32
BF16 = jnp.bfloat16

LN_EPS = 1e-5
N_BUCKETS = 32
MAX_DISTANCE = 128
CONV_W = 3
NEG = -1e30
SAMPLE_ROWS = 8
V7X_VMEM_LIMIT = 56 * 1024 * 1024


def _bucket_thresholds():
    max_exact = N_BUCKETS // 2
    d = np.arange(0, 4 * MAX_DISTANCE)
    large = max_exact + (np.log(np.maximum(d, 1) / max_exact) / math.log(MAX_DISTANCE / max_exact)
                         * (N_BUCKETS - max_exact)).astype(np.int64)
    bucket = np.where(d < max_exact, d, np.minimum(large, N_BUCKETS - 1))
    return tuple(int(d[bucket >= k].min()) for k in range(1, N_BUCKETS))


_THR = _bucket_thresholds()
_FAR = _THR[-1]


def _cparams(sem):
    return pltpu.CompilerParams(dimension_semantics=sem, vmem_limit_bytes=V7X_VMEM_LIMIT)


def _nt(a, b):
    return lax.dot_general(a, b, (((1,), (1,)), ((), ())), preferred_element_type=F32)


def _tn(a, b):
    return lax.dot_general(a, b, (((0,), (0,)), ((), ())), preferred_element_type=F32)


def _sigmoid(x):
    return 1.0 / (1.0 + jnp.exp(-x))


def _pbias_kernel(rb_ref, o_ref, *, T):
    h = pl.program_id(0)
    r = lax.broadcasted_iota(jnp.int32, (T, T), 0)
    c = lax.broadcasted_iota(jnp.int32, (T, T), 1)
    for t, off in enumerate((0, T)):
        d = r - c + off
        bias = jnp.full((T, T), rb_ref[0, h], F32)
        for k in range(1, N_BUCKETS):
            bias = jnp.where(d >= _THR[k - 1], rb_ref[k, h], bias)
        if off == 0:
            bias = jnp.where(d >= 0, bias, NEG)
        o_ref[0, t] = bias


def _prompt_bias(rel_bias, T):
    H = rel_bias.shape[1]
    return pl.pallas_call(
        functools.partial(_pbias_kernel, T=T),
        grid=(H,),
        in_specs=[pl.BlockSpec(memory_space=pltpu.SMEM)],
        out_specs=pl.BlockSpec((1, 2, T, T), lambda h: (h, 0, 0, 0)),
        out_shape=jax.ShapeDtypeStruct((H, 2, T, T), F32),
        compiler_params=_cparams(("arbitrary",)),
        name="prompt_bias",
    )(rel_bias)


def _dbias_kernel(rb_ref, d_ref, n_ref, *, P, G):
    h = pl.program_id(0)
    NQ = SAMPLE_ROWS
    R = 2 * NQ

    def build(d):
        bias = jnp.full(d.shape, rb_ref[0, h], F32)
        for k in range(1, N_BUCKETS):
            bias = jnp.where(d >= _THR[k - 1], rb_ref[k, h], bias)
        return bias

    r = lax.broadcasted_iota(jnp.int32, (R, P), 0)
    c = lax.broadcasted_iota(jnp.int32, (R, P), 1)
    qi = r % NQ
    far = jnp.full((R, P), rb_ref[N_BUCKETS - 1, h], F32)
    last = build(P + qi - c)
    dn = qi - c
    new = jnp.where(dn >= 0, build(jnp.maximum(dn, 0)), NEG)
    d_ref[0] = jnp.concatenate([far] * G, axis=1)
    d_ref[1] = jnp.concatenate([far] * (G - 1) + [last], axis=1)
    n_ref[...] = new


def _decode_bias(rel_bias, P, G):
    H = rel_bias.shape[1]
    R = 2 * SAMPLE_ROWS
    return pl.pallas_call(
        functools.partial(_dbias_kernel, P=P, G=G),
        grid=(H,),
        in_specs=[pl.BlockSpec(memory_space=pltpu.SMEM)],
        out_specs=[pl.BlockSpec((2, R, G * P), lambda h: (0, h, 0)),
                   pl.BlockSpec((R, P), lambda h: (h, 0))],
        out_shape=(jax.ShapeDtypeStruct((2, H * R, G * P), F32), jax.ShapeDtypeStruct((H * R, P), F32)),
        compiler_params=_cparams(("arbitrary",)),
        name="decode_bias",
    )(rel_bias)


def _inproj_kernel(x_ref, w_ref, wg_ref, pb_ref, k_ref, v_ref, om_ref, g_ref, *, H, NG):
    j = pl.program_id(1)
    x = x_ref[...].astype(BF16)
    acc = jnp.dot(x, w_ref[...], preferred_element_type=F32)
    bm, W = acc.shape
    DK = W // H

    def store_heads(ref):
        for h in range(H):
            ref[pl.ds(h, bm, stride=H), :] = acc[:, h * DK:(h + 1) * DK]

    @pl.when(j <= 5)
    def _():
        pb_ref[...] = acc.astype(pb_ref.dtype)

    @pl.when(j == 1)
    def _():
        store_heads(k_ref)

    @pl.when(j == 2)
    def _():
        store_heads(v_ref)

    @pl.when(j == 6)
    def _():
        om_ref[...] = acc

    @pl.when(j == 0)
    def _():
        gates = jnp.dot(x, wg_ref[...], preferred_element_type=F32)
        if bm % 128:
            gates = jnp.concatenate([gates, jnp.zeros((128 - bm % 128, gates.shape[1]), F32)], axis=0)
        g_ref[0] = gates.T[:NG, :bm]


def _inproj(x, w, layer, wg, *, bm, rows_per_seq, pb_dtype, W, H, NG):
    M, D = x.shape
    DK = W // H
    tpb = rows_per_seq // bm
    grid = (M // bm, 7)
    return pl.pallas_call(
        functools.partial(_inproj_kernel, H=H, NG=NG),
        grid=grid,
        in_specs=[pl.BlockSpec((bm, D), lambda i, j: (i, 0)),
                  pl.BlockSpec((None, D, W), lambda i, j: (layer, 0, j)),
                  pl.BlockSpec(wg.shape, lambda i, j: (0, 0))],
        out_specs=[pl.BlockSpec((bm, W), lambda i, j: (i, jnp.minimum(j, 5))),
                   pl.BlockSpec((bm * H, DK), lambda i, j: (i, 0)),
                   pl.BlockSpec((bm * H, DK), lambda i, j: (i, 0)),
                   pl.BlockSpec((bm, W), lambda i, j: (i, 0)),
                   pl.BlockSpec((1, NG, bm), lambda i, j: (i // tpb, 0, i % tpb))],
        out_shape=[jax.ShapeDtypeStruct((M, 6 * W), pb_dtype),
                   jax.ShapeDtypeStruct((M * H, DK), F32),
                   jax.ShapeDtypeStruct((M * H, DK), F32),
                   jax.ShapeDtypeStruct((M, W), F32),
                   jax.ShapeDtypeStruct((M // rows_per_seq, NG, rows_per_seq), F32)],
        compiler_params=_cparams(("arbitrary", "arbitrary")),
        name="in_proj",
    )(x, w, wg)


def _deepnorm(res, y, g_ref, b_ref, o32_ref, o16_ref, alpha):
    z = alpha * res + y
    mu = jnp.mean(z, axis=-1, keepdims=True)
    zc = z - mu
    var = jnp.mean(zc * zc, axis=-1, keepdims=True)
    o = zc * lax.rsqrt(var + LN_EPS) * g_ref[...] + b_ref[...]
    o32_ref[...] = o
    o16_ref[...] = o.astype(BF16)


def _outproj_ln_kernel(x1_ref, x2_ref, w1_ref, w2_ref, r_ref, g_ref, b_ref, o32_ref, o16_ref, *, alpha):
    mix = (jnp.dot(x1_ref[...].astype(BF16), w1_ref[...], preferred_element_type=F32)
           + jnp.dot(x2_ref[...].astype(BF16), w2_ref[...], preferred_element_type=F32))
    _deepnorm(r_ref[...], mix, g_ref, b_ref, o32_ref, o16_ref, alpha)


def _outproj_ln(x1, x2, w, layer, res, g, b, *, alpha, bm):
    M, K1 = x1.shape
    N = w.shape[2]
    row = pl.BlockSpec((bm, N), lambda i: (i, 0))
    vec = pl.BlockSpec((1, N), lambda i: (0, 0))
    return pl.pallas_call(
        functools.partial(_outproj_ln_kernel, alpha=alpha),
        grid=(M // bm,),
        in_specs=[pl.BlockSpec((bm, K1), lambda i: (i, 0)),
                  pl.BlockSpec((bm, K1), lambda i: (i, 0)),
                  pl.BlockSpec((None, K1, N), lambda i: (layer, 0, 0)),
                  pl.BlockSpec((None, K1, N), lambda i: (layer, 1, 0)),
                  row, vec, vec],
        out_specs=[row, row],
        out_shape=[jax.ShapeDtypeStruct((M, N), F32), jax.ShapeDtypeStruct((M, N), BF16)],
        compiler_params=_cparams(("arbitrary",)),
        name="out_proj_ln",
    )(x1, x2, w, w, res, g.reshape(1, N), b.reshape(1, N))


def _down_ln_kernel(x_ref, w_ref, r_ref, g_ref, b_ref, o32_ref, o16_ref, acc_ref, *, alpha, nk):
    k = pl.program_id(1)
    d = jnp.dot(x_ref[...].astype(BF16), w_ref[...], preferred_element_type=F32)

    @pl.when(k == 0)
    def _():
        acc_ref[...] = d

    @pl.when(k > 0)
    def _():
        acc_ref[...] += d

    @pl.when(k == nk - 1)
    def _():
        _deepnorm(r_ref[...], acc_ref[...], g_ref, b_ref, o32_ref, o16_ref, alpha)


def _down_ln(x, w, layer, res, g, b, *, alpha, bm, bk):
    M, K = x.shape
    N = w.shape[2]
    nk = K // bk
    row = pl.BlockSpec((bm, N), lambda i, k: (i, 0))
    vec = pl.BlockSpec((1, N), lambda i, k: (0, 0))
    return pl.pallas_call(
        functools.partial(_down_ln_kernel, alpha=alpha, nk=nk),
        grid=(M // bm, nk),
        in_specs=[pl.BlockSpec((bm, bk), lambda i, k: (i, k)),
                  pl.BlockSpec((None, bk, N), lambda i, k: (layer, k, 0)),
                  row, vec, vec],
        out_specs=[row, row],
        out_shape=[jax.ShapeDtypeStruct((M, N), F32), jax.ShapeDtypeStruct((M, N), BF16)],
        scratch_shapes=[pltpu.VMEM((bm, N), F32)],
        compiler_params=_cparams(("arbitrary", "arbitrary")),
        name="down_proj_ln",
    )(x, w, res, g.reshape(1, N), b.reshape(1, N))


def _upconv_kernel(x_ref, wa_ref, wg_ref, cwa_ref, cwg_ref, cba_ref, cbg_ref, c0a_ref, c0g_ref,
                   act_ref, ta_ref, tg_ref, wa16_ref, wg16_ref, ha_ref, hg_ref, *, bm, cols, seq_rows):
    @pl.when((pl.program_id(1) == 0) & (pl.program_id(2) == 0))
    def _():
        wa16_ref[...] = wa_ref[...].astype(BF16)
        wg16_ref[...] = wg_ref[...].astype(BF16)

    x = x_ref[...].astype(BF16)
    fresh = seq_rows < bm
    first = pl.program_id(2) == 0
    if fresh:
        rmod = lax.broadcasted_iota(jnp.int32, (bm, cols), 0) & (seq_rows - 1)
    else:
        rid = lax.broadcasted_iota(jnp.int32, (8, cols), 0)
    for c in range(act_ref.shape[1] // cols):
        cs = slice(c * cols, (c + 1) * cols)
        conv = []
        for w_ref, cw_ref, cb_ref, c0_ref, t_ref, h_ref in ((wa16_ref, cwa_ref, cba_ref, c0a_ref, ta_ref, ha_ref),
                                                            (wg16_ref, cwg_ref, cbg_ref, c0g_ref, tg_ref, hg_ref)):
            u = jnp.dot(x, w_ref[:, cs], preferred_element_type=F32)
            u1 = pltpu.roll(u, 1, axis=0)
            u2 = pltpu.roll(u, 2, axis=0)
            if fresh:
                u1 = jnp.where(rmod == 0, c0_ref[0, :, cs], u1)
                u2 = jnp.where(rmod < 2, c0_ref[1, :, cs], u2)
                t_ref[:, cs] = u
            else:
                c0 = c0_ref[0, :, cs]
                prev = h_ref[:, cs]
                p2 = jnp.where(first, c0[0:1], prev[6:7])
                p1 = jnp.where(first, c0[1:2], prev[7:8])
                h1 = jnp.where(rid == 0, p1, u1[:8])
                h2 = jnp.where(rid == 0, p2, jnp.where(rid == 1, p1, u2[:8]))
                u1 = jnp.concatenate([h1, u1[8:]], axis=0)
                u2 = jnp.concatenate([h2, u2[8:]], axis=0)
                h_ref[:, cs] = u[bm - 8:]
                t_ref[0, :, cs] = u[bm - 8:]
            cw = cw_ref[:, cs]
            conv.append(cb_ref[:, cs] + cw[0:1] * u2 + cw[1:2] * u1 + cw[2:3] * u)

        a_br, g_br = conv
        act_ref[:, cs] = (g_br * _sigmoid(g_br) * a_br).astype(act_ref.dtype)


def _upconv(x, w_up, layer, conv_w, conv_b, conv0, *, bm, bn, seq_rows, act_dtype):
    M, D = x.shape
    F = w_up.shape[2] // 2
    NT = F // bn
    fresh = seq_rows < bm
    nb, nmt = (M // bm, 1) if fresh else (M // seq_rows, seq_rows // bm)
    cb = conv_b.reshape(1, 2 * F)
    half = lambda off: (lambda j, b, i: (0, off + j))
    wsel = lambda off: (lambda j, b, i: (layer, 0, off + j))
    rows = lambda j, b, i: (b * nmt + i, j)
    if fresh:
        c0_spec = lambda off: pl.BlockSpec((2, bm, bn), lambda j, b, i: (0, b, off + j))
        tail_spec = pl.BlockSpec((bm, bn), rows)
        tail_shape = jax.ShapeDtypeStruct((M, F), F32)
    else:
        c0_spec = lambda off: pl.BlockSpec((1, CONV_W - 1, bn), lambda j, b, i: (b, 0, off + j))
        tail_spec = pl.BlockSpec((1, 8, bn), lambda j, b, i: (b, 0, j))
        tail_shape = jax.ShapeDtypeStruct((nb, 8, F), F32)
    return pl.pallas_call(
        functools.partial(_upconv_kernel, bm=bm, cols=math.gcd(bn, 256), seq_rows=seq_rows),
        grid=(NT, nb, nmt),
        in_specs=[pl.BlockSpec((bm, D), lambda j, b, i: (b * nmt + i, 0)),
                  pl.BlockSpec((None, D, bn), wsel(0)), pl.BlockSpec((None, D, bn), wsel(NT)),
                  pl.BlockSpec((CONV_W, bn), half(0)), pl.BlockSpec((CONV_W, bn), half(NT)),
                  pl.BlockSpec((1, bn), half(0)), pl.BlockSpec((1, bn), half(NT)),
                  c0_spec(0), c0_spec(NT)],
        out_specs=[pl.BlockSpec((bm, bn), rows), tail_spec, tail_spec],
        out_shape=[jax.ShapeDtypeStruct((M, F), act_dtype), tail_shape, tail_shape],
        scratch_shapes=[pltpu.VMEM((D, bn), BF16), pltpu.VMEM((D, bn), BF16),
                        pltpu.VMEM((8, bn), F32), pltpu.VMEM((8, bn), F32)],
        compiler_params=_cparams(("arbitrary", "arbitrary", "arbitrary")),
        name="up_conv_gate",
    )(x, w_up, w_up, conv_w, conv_w, cb, cb, conv0, conv0)


def _lambda(lv_ref, lam_init):
    lv = lv_ref[...]
    s1 = jnp.sum(lv[0:1] * lv[1:2], axis=1, keepdims=True)
    s2 = jnp.sum(lv[2:3] * lv[3:4], axis=1, keepdims=True)
    return jnp.exp(s1) - jnp.exp(s2) + lam_init


def _pattn_kernel(li_ref, rb_ref, q_ref, k_ref, v_ref, pbias_ref, lv_ref, sw_ref, o_ref,
                  m_ref, l_ref, acc_ref, *, T, DH):
    h = pl.program_id(1)
    i = pl.program_id(2)
    lane = lax.broadcasted_iota(jnp.int32, q_ref.shape, 1)
    q = q_ref[...].astype(F32) * (DH ** -0.5)
    qs = [jnp.where(lane < DH, q, 0.0).astype(BF16), jnp.where(lane >= DH, q, 0.0).astype(BF16)]
    m_ref[...] = jnp.full(m_ref.shape, NEG, F32)
    l_ref[...] = jnp.zeros(l_ref.shape, F32)
    acc_ref[...] = jnp.zeros(acc_ref.shape, F32)
    reps = T // m_ref.shape[2]

    def tile(j, bias):
        start = pl.multiple_of(j * T, T)
        kt = k_ref[pl.ds(start, T), :]
        vt = v_ref[pl.ds(start, T), :]
        for mp in range(2):
            s = _nt(qs[mp], kt) + bias
            m_old = m_ref[mp]
            m_new = jnp.maximum(m_old, jnp.max(s, axis=1, keepdims=True))
            alpha = jnp.exp(m_old - m_new)
            p = jnp.exp(s - jnp.concatenate([m_new] * reps, axis=1))
            l_ref[mp] = alpha * l_ref[mp] + jnp.sum(p, axis=1, keepdims=True)
            acc_ref[mp] = alpha * acc_ref[mp] + jnp.dot(p.astype(BF16), vt, preferred_element_type=F32)
            m_ref[mp] = m_new

    far_bias = rb_ref[N_BUCKETS - 1, h]

    def far_body(j, carry):
        tile(j, far_bias)
        return carry

    lax.fori_loop(0, jnp.maximum(i - 1, 0), far_body, 0)

    @pl.when(i >= 1)
    def _():
        tile(i - 1, pbias_ref[0, 1])

    tile(i, pbias_ref[0, 0])

    lam_init = li_ref[0]
    lam = _lambda(lv_ref, lam_init)
    o = acc_ref[0] / l_ref[0] - lam * (acc_ref[1] / l_ref[1])
    o = o * lax.rsqrt(jnp.mean(o * o, axis=-1, keepdims=True) + LN_EPS) * sw_ref[...] * (1.0 - lam_init)
    o_ref[...] = o.astype(o_ref.dtype)


def _prompt_attention(pb, rel_bias, pbias, lam_vecs, subln_w, lam_init, *, B, L, H, DH, T):
    M = pb.shape[0]
    DV = 2 * DH
    nq = L // T
    return pl.pallas_call(
        functools.partial(_pattn_kernel, T=T, DH=DH),
        grid=(B, H, nq),
        in_specs=[pl.BlockSpec(memory_space=pltpu.SMEM),
                  pl.BlockSpec(memory_space=pltpu.SMEM),
                  pl.BlockSpec((T, DV), lambda b, h, i: (b * nq + i, h)),
                  pl.BlockSpec((L, DV), lambda b, h, i: (b, H + h)),
                  pl.BlockSpec((L, DV), lambda b, h, i: (b, 2 * H + h)),
                  pl.BlockSpec((1, 2, T, T), lambda b, h, i: (h, 0, 0, 0)),
                  pl.BlockSpec((4, DH), lambda b, h, i: (0, 0)),
                  pl.BlockSpec((1, DV), lambda b, h, i: (0, 0))],
        out_specs=pl.BlockSpec((T, DV), lambda b, h, i: (b * nq + i, h)),
        out_shape=jax.ShapeDtypeStruct((M, H * DV), BF16),
        scratch_shapes=[pltpu.VMEM((2, T, DV), F32), pltpu.VMEM((2, T, DV), F32), pltpu.VMEM((2, T, DV), F32)],
        compiler_params=_cparams(("arbitrary", "arbitrary", "arbitrary")),
        name="prompt_attention",
    )(lam_init, rel_bias, pb, pb, pb, pbias, lam_vecs, subln_w.reshape(1, DV))


def _dattn_kernel(pt_ref, li_ref, q_ref, kn_ref, vn_ref, db_ref, nb_ref, lv_ref, sw_ref, *rest,
                  G, P, H, DH, nsteps):
    del pt_ref
    k_refs = rest[:G]
    v_refs = rest[G:2 * G]
    o_ref, qb_ref, m_ref, l_ref, acc_ref = rest[2 * G:]
    NQ = SAMPLE_ROWS
    DV = 2 * DH
    WA = H * DV
    RH = 2 * NQ
    s_id = pl.program_id(1)

    @pl.when(s_id == 0)
    def _():
        q = q_ref[...] * (DH ** -0.5)
        lane = lax.broadcasted_iota(jnp.int32, (NQ, WA), 1)
        blocks = []
        for h in range(H):
            for mp in range(2):
                lo = h * DV + mp * DH
                blocks.append(jnp.where((lane >= lo) & (lane < lo + DH), q, 0.0))
        qb_ref[...] = jnp.concatenate(blocks, axis=0).astype(BF16)
        m_ref[...] = jnp.full(m_ref.shape, NEG, F32)
        l_ref[...] = jnp.zeros(l_ref.shape, F32)
        acc_ref[...] = jnp.zeros(acc_ref.shape, F32)

    def update(kc, v_heads, bias):
        s = _nt(qb_ref[...], kc) + bias
        m_old = m_ref[...]
        m_new = jnp.maximum(m_old, jnp.max(s, axis=1, keepdims=True))
        alpha = jnp.exp(m_old - m_new)
        p = jnp.exp(s - jnp.concatenate([m_new] * (s.shape[1] // DV), axis=1))
        l_ref[...] = alpha * l_ref[...] + jnp.sum(p, axis=1, keepdims=True)
        pb = p.astype(BF16)
        pv = jnp.concatenate([jnp.dot(pb[h * RH:(h + 1) * RH], v_heads[h], preferred_element_type=F32)
                              for h in range(H)], axis=0)
        acc_ref[...] = alpha * acc_ref[...] + pv
        m_ref[...] = m_new

    def head_rows(refs, h):
        return jnp.concatenate([r[pl.ds(h, P, stride=H), :] for r in refs], axis=0).astype(BF16)

    is_last = (s_id == nsteps - 1).astype(jnp.int32)
    kc = jnp.concatenate([head_rows(k_refs, h) for h in range(H)], axis=1)
    update(kc, [head_rows(v_refs, h) for h in range(H)], db_ref[is_last])

    @pl.when(s_id == nsteps - 1)
    def _():
        lam_init = li_ref[0]
        lam = _lambda(lv_ref, lam_init)
        pad = jnp.zeros((P - NQ, WA), F32)
        kn = jnp.concatenate([kn_ref[...], pad], axis=0).astype(BF16)
        vn = jnp.concatenate([vn_ref[...], pad], axis=0).astype(BF16)
        update(kn, [vn[:, h * DV:(h + 1) * DV] for h in range(H)], nb_ref[...])
        a = acc_ref[...] / l_ref[...]
        for h in range(H):
            o = a[h * RH:h * RH + NQ] - lam * a[h * RH + NQ:(h + 1) * RH]
            o = o * lax.rsqrt(jnp.mean(o * o, axis=-1, keepdims=True) + LN_EPS) * sw_ref[...] * (1.0 - lam_init)
            o_ref[:, h * DV:(h + 1) * DV] = o


def _decode_attention(pb_s, cache_k, cache_v, page_idx, dbias, nbias, lam_vecs, subln_w, lam_init,
                      *, BS, NP, G, P, H, DH):
    NQ = SAMPLE_ROWS
    DV = 2 * DH
    WA = H * DV
    R = 2 * NQ * H
    nsteps = NP // G
    row = lambda col: pl.BlockSpec((NQ, WA), lambda b, s, pt: (b, col))
    full = lambda shape: pl.BlockSpec(shape, lambda b, s, pt: (0,) * len(shape))
    page = lambda g: pl.BlockSpec((None, P * H, DV), lambda b, s, pt: (pt[b * NP + s * G + g], 0, 0))
    grid_spec = pltpu.PrefetchScalarGridSpec(
        num_scalar_prefetch=1,
        grid=(BS, nsteps),
        in_specs=[pl.BlockSpec(memory_space=pltpu.SMEM),
                  row(0), row(1), row(2),
                  full((2, R, G * P)), full((R, P)), full((4, DH)), full((1, DV))]
                 + [page(g) for g in range(G)] + [page(g) for g in range(G)],
        out_specs=pl.BlockSpec((NQ, WA), lambda b, s, pt: (b, 0)),
        scratch_shapes=[pltpu.VMEM((R, WA), BF16), pltpu.VMEM((R, DV), F32), pltpu.VMEM((R, DV), F32),
                        pltpu.VMEM((R, DV), F32)])
    return pl.pallas_call(
        functools.partial(_dattn_kernel, G=G, P=P, H=H, DH=DH, nsteps=nsteps),
        grid_spec=grid_spec,
        out_shape=jax.ShapeDtypeStruct((BS * NQ, WA), F32),
        compiler_params=_cparams(("arbitrary", "arbitrary")),
        name="decode_attention",
    )(page_idx, lam_init, pb_s, pb_s, pb_s, dbias, nbias, lam_vecs, subln_w.reshape(1, DV),
      *([cache_k] * G), *([cache_v] * G))


def _log_sigmoid(x):
    return jnp.minimum(x, 0.0) - jnp.log1p(jnp.exp(-jnp.abs(x)))


def _mlstm_kernel(*refs, cs, rows, valid, zero_init, H, DM, nc):
    if zero_init:
        g_ref, gb_ref, q_ref, k_ref, v_ref, om_ref, bo_ref, mw_ref = refs[:8]
        rest = refs[8:]
    else:
        g_ref, gb_ref, q_ref, k_ref, v_ref, om_ref, bo_ref, mw_ref, c0_ref, n0_ref, m0_ref = refs[:11]
        rest = refs[11:]
    o_ref, co_ref, no_ref, mo_ref, c_s, n_s, m_s = rest
    c = pl.program_id(1)
    scale = DM ** -0.5

    @pl.when(c == 0)
    def _():
        if zero_init:
            c_s[...] = jnp.zeros(c_s.shape, F32)
            n_s[...] = jnp.zeros(n_s.shape, F32)
            m_s[...] = jnp.zeros(m_s.shape, F32)
        else:
            c_s[...] = c0_ref[0]
            n_s[...] = n0_ref[0]
            m_s[...] = m0_ref[0]

    g = g_ref[0] + gb_ref[...]
    li_all = g[:H]
    lf_all = _log_sigmoid(g[H:])
    if valid < cs:
        col = lax.broadcasted_iota(jnp.int32, (H, cs), 1)
        li_all = jnp.where(col < valid, li_all, NEG)
        lf_all = jnp.where(col < valid, lf_all, 0.0)
    rr = lax.broadcasted_iota(jnp.int32, (cs, cs), 0)
    cc = lax.broadcasted_iota(jnp.int32, (cs, cs), 1)
    causal = cc <= rr
    eye = cc == rr
    upper = (rr <= cc).astype(F32)
    b_all = jnp.dot(lf_all, upper, preferred_element_type=F32, precision=lax.Precision.HIGHEST)

    def pad_rows(x):
        if rows == cs:
            return x
        return jnp.concatenate([x, jnp.zeros((cs - rows, x.shape[1]), x.dtype)], axis=0)

    ones_dm = jnp.ones((cs, DM), BF16)

    def to_col(row):
        return jnp.sum(jnp.where(eye, row, 0.0), axis=1, keepdims=True)

    for h in range(H):
        sl = slice(h * DM, (h + 1) * DM)
        q = pad_rows(q_ref[:, sl])
        k = pad_rows(k_ref[:, sl])
        v = pad_rows(v_ref[:, sl])
        qb, kb, vb = q.astype(BF16), k.astype(BF16), v.astype(BF16)
        k32 = k.astype(F32)
        C = c_s[h]
        n = n_s[h:h + 1, :]
        m_prev = m_s[h:h + 1, :]
        li = li_all[h:h + 1, :]
        b_row = b_all[h:h + 1, :]
        r_row = li - b_row
        b_col = jnp.broadcast_to(to_col(b_row), (cs, DM))
        r_col = jnp.broadcast_to(to_col(r_row), (cs, DM))
        dmat = jnp.where(causal, jnp.concatenate([b_col] * (cs // DM), axis=1) + r_row, NEG)
        inter = b_col + m_prev
        mt = jnp.maximum(jnp.max(dmat, axis=1, keepdims=True), inter)
        w = (jnp.exp(dmat - jnp.concatenate([mt] * (cs // DM), axis=1)) * (_nt(qb, kb) * scale)).astype(BF16)
        a = jnp.exp(inter - mt)
        num = a * _nt(qb, C.astype(BF16)) + jnp.dot(w, vb, preferred_element_type=F32)
        qn = _nt(qb, jnp.broadcast_to(n, (DM, DM)).astype(BF16))
        den = a * qn + jnp.dot(w, ones_dm, preferred_element_type=F32)
        hh = num / jnp.maximum(jnp.abs(den), jnp.exp(-mt))

        bL = b_row[:, cs - 1:cs]
        m_new = jnp.maximum(bL + m_prev, jnp.max(bL + r_row, axis=1, keepdims=True))
        decay = jnp.exp(bL + m_prev - m_new)
        kw = k32 * jnp.exp(bL + r_col - m_new)
        c_s[h] = decay * C + scale * _tn(vb, kw.astype(BF16))
        n_s[h:h + 1, :] = decay * n + scale * jnp.sum(kw, axis=0, keepdims=True)
        m_s[h:h + 1, :] = jnp.broadcast_to(m_new, (1, DM))

        mu = jnp.mean(hh, axis=-1, keepdims=True)
        hc = hh - mu
        var = jnp.mean(hc * hc, axis=-1, keepdims=True)
        hn = hc * lax.rsqrt(var + LN_EPS) * mw_ref[:, sl]
        og = _sigmoid(pad_rows(om_ref[:, sl]) + bo_ref[:, sl])
        o_ref[:, sl] = (og * hn)[:rows].astype(o_ref.dtype)

    @pl.when(c == nc - 1)
    def _():
        co_ref[0] = c_s[...]
        no_ref[0] = n_s[...]
        mo_ref[0] = m_s[...]


def _mlstm(gates, gate_bias, pb, om32, b_o, mhn_w, state, *, nseq, cs, rows, valid, H, DM, out_dtype):
    WM = H * DM
    M = pb.shape[0]
    nc = gates.shape[2] // cs
    zero_init = state is None
    blk = lambda col: pl.BlockSpec((rows, WM), lambda b, c: (b * nc + c, col))
    vec = pl.BlockSpec((1, WM), lambda b, c: (0, 0))
    st_c = pl.BlockSpec((1, H, DM, DM), lambda b, c: (b, 0, 0, 0))
    st_v = pl.BlockSpec((1, H, DM), lambda b, c: (b, 0, 0))
    in_specs = [pl.BlockSpec((1, 2 * H, cs), lambda b, c: (b, 0, c)),
                pl.BlockSpec((2 * H, 1), lambda b, c: (0, 0)),
                blk(3), blk(4), blk(5), blk(0), vec, vec]
    args = [gates, gate_bias, pb, pb, pb, om32, b_o.reshape(1, WM), mhn_w.reshape(1, WM)]
    if not zero_init:
        in_specs += [st_c, st_v, st_v]
        args += list(state)
    return pl.pallas_call(
        functools.partial(_mlstm_kernel, cs=cs, rows=rows, valid=valid, zero_init=zero_init, H=H, DM=DM, nc=nc),
        grid=(nseq, nc),
        in_specs=in_specs,
        out_specs=[blk(0), st_c, st_v, st_v],
        out_shape=[jax.ShapeDtypeStruct((M, WM), out_dtype),
                   jax.ShapeDtypeStruct((nseq, H, DM, DM), F32),
                   jax.ShapeDtypeStruct((nseq, H, DM), F32),
                   jax.ShapeDtypeStruct((nseq, H, DM), F32)],
        scratch_shapes=[pltpu.VMEM((H, DM, DM), F32), pltpu.VMEM((H, DM), F32), pltpu.VMEM((H, DM), F32)],
        compiler_params=_cparams(("arbitrary", "arbitrary")),
        name="mlstm",
    )(*args)


def _largest_divisor(n, cap, mult):
    best = None
    for t in range(mult, min(n, cap) + 1, mult):
        if n % t == 0:
            best = t
    assert best is not None, (n, cap, mult)
    return best


def _tiles(L, D, F):
    return dict(
        attn_t=_largest_divisor(L, 512, 128),
        mlstm_cs=_largest_divisor(L, 256, 128),
        inproj_bm=_largest_divisor(L, 512, 8),
        out_bm=_largest_divisor(L, 512, 8),
        up_bm=_largest_divisor(L, 1024, 8),
        up_bn=_largest_divisor(F, 512, 128),
        up_bn_sample=_largest_divisor(F, 704, 128),
        down_bm=_largest_divisor(L, 512, 8),
        down_bk=_largest_divisor(F, 1408, 128),
        pages_per_step=8,
    )


def kernel(x_prompt, x_sample, cache_k, cache_v, page_table, state_C, state_n, state_m, state_conv, rel_bias, w_in, b_i, b_f, b_o, lambda_q1, lambda_k1, lambda_q2, lambda_k2, subln_w, mhn_w, w_out, ln1_g, ln1_b, w_up, conv_w, conv_b, w_down, ln2_g, ln2_b):
    B, L, D = x_prompt.shape
    BS, LS, _ = x_sample.shape
    DEPTH, NPOOL, P, HA, DK = cache_k.shape
    DH = DK // 2
    WA = HA * DK
    HM, DM = state_C.shape[2], state_C.shape[3]
    WM = HM * DM
    F = w_down.shape[1]
    NP = page_table.shape[1]
    NQ = SAMPLE_ROWS
    assert WA == WM and w_in.shape[2] == 3 * WA + 4 * WM + 2 * HM
    assert LS <= NQ and LS >= CONV_W - 1 and P >= _FAR and HA == 8 and HM == 8
    t = _tiles(L, D, F)
    T = t["attn_t"]
    assert T >= _FAR
    G = min(t["pages_per_step"], NP)
    assert NP % G == 0
    alpha = (2 * DEPTH) ** 0.25

    pbias = _prompt_bias(rel_bias, T)
    dbias, nbias = _decode_bias(rel_bias, P, G)
    cache_k2 = cache_k.reshape(DEPTH * NPOOL, P * HA, DK)
    cache_v2 = cache_v.reshape(DEPTH * NPOOL, P * HA, DK)
    zconv = jnp.zeros((B, CONV_W - 1, 2 * F), F32)
    sample_cs = 128

    xp32 = x_prompt.reshape(B * L, D)
    xp16 = xp32.astype(BF16)
    xs32 = jnp.pad(x_sample, ((0, 0), (0, NQ - LS), (0, 0))).reshape(BS * NQ, D)
    xs16 = xs32.astype(BF16)
    wi16 = w_in.astype(BF16)
    wo16 = w_out.astype(BF16)
    wd16 = w_down.astype(BF16)

    outs = {k: [] for k in ("kp", "vp", "ks", "vs", "Cp", "np", "mp", "Cs", "ns", "ms", "cp", "cs")}
    for l in range(DEPTH):
        lam0 = 0.8 - 0.6 * math.exp(-0.3 * l)
        lam_init = jnp.full((1,), lam0, F32)
        lam_vecs = jnp.stack([lambda_q1[l], lambda_k1[l], lambda_q2[l], lambda_k2[l]])
        wg = jnp.pad(w_in[l, :, 3 * WA + 4 * WM:], ((0, 0), (0, 128 - 2 * HM))).astype(BF16)
        gate_bias = jnp.concatenate([b_i[l], b_f[l]]).reshape(2 * HM, 1)
        page_idx = (page_table + l * NPOOL).reshape(-1).astype(jnp.int32)

        pb, k32, v32, om32, gates = _inproj(xp16, wi16, l, wg, bm=t["inproj_bm"], rows_per_seq=L, pb_dtype=BF16,
                                            W=WA, H=HA, NG=2 * HM)
        oa = _prompt_attention(pb, rel_bias, pbias, lam_vecs, subln_w[l], lam_init, B=B, L=L, H=HA, DH=DH, T=T)
        om, Cn, nn, mn = _mlstm(gates, gate_bias, pb, om32, b_o[l], mhn_w[l], None, nseq=B, cs=t["mlstm_cs"],
                                rows=t["mlstm_cs"], valid=t["mlstm_cs"], H=HM, DM=DM, out_dtype=BF16)
        h32, h16 = _outproj_ln(oa, om, wo16, l, xp32, ln1_g[l], ln1_b[l], alpha=alpha, bm=t["out_bm"])
        act, ta, tg = _upconv(h16, w_up, l, conv_w[l], conv_b[l], zconv, bm=t["up_bm"], bn=t["up_bn"],
                              seq_rows=L, act_dtype=BF16)
        xp32, xp16 = _down_ln(act, wd16, l, h32, ln2_g[l], ln2_b[l], alpha=alpha, bm=t["down_bm"], bk=t["down_bk"])
        outs["kp"].append(k32.reshape(B, L, HA, DK))
        outs["vp"].append(v32.reshape(B, L, HA, DK))
        outs["Cp"].append(Cn)
        outs["np"].append(nn)
        outs["mp"].append(mn[:, :, 0])
        outs["cp"].append(jnp.concatenate([ta[:, 8 - (CONV_W - 1):], tg[:, 8 - (CONV_W - 1):]], axis=-1))

        MS = BS * NQ
        pb, k32, v32, om32, gates = _inproj(xs16, wi16, l, wg, bm=MS, rows_per_seq=MS, pb_dtype=F32,
                                            W=WA, H=HA, NG=2 * HM)
        oa = _decode_attention(pb, cache_k2, cache_v2, page_idx, dbias, nbias, lam_vecs, subln_w[l], lam_init,
                               BS=BS, NP=NP, G=G, P=P, H=HA, DH=DH)
        gs = gates.reshape(2 * HM, BS, NQ).transpose(1, 0, 2)
        gs = jnp.pad(gs, ((0, 0), (0, 0), (0, sample_cs - NQ)))
        state = (state_C[l], state_n[l], jnp.broadcast_to(state_m[l][:, :, None], (BS, HM, DM)))
        om, Cn, nn, mn = _mlstm(gs, gate_bias, pb, om32, b_o[l], mhn_w[l], state, nseq=BS, cs=sample_cs,
                                rows=NQ, valid=LS, H=HM, DM=DM, out_dtype=F32)
        h32, h16 = _outproj_ln(oa, om, wo16, l, xs32, ln1_g[l], ln1_b[l], alpha=alpha, bm=MS)
        sc = state_conv[l]
        c0x = jnp.stack([jnp.pad(sc[:, CONV_W - 2:], ((0, 0), (0, NQ - 1), (0, 0))),
                         jnp.pad(sc, ((0, 0), (0, NQ - (CONV_W - 1)), (0, 0)))]).reshape(2, MS, 2 * F)
        act, ta, tg = _upconv(h16, w_up, l, conv_w[l], conv_b[l], c0x, bm=MS, bn=t["up_bn_sample"],
                              seq_rows=NQ, act_dtype=F32)
        ta, tg = ta.reshape(BS, NQ, F), tg.reshape(BS, NQ, F)
        xs32, xs16 = _down_ln(act, wd16, l, h32, ln2_g[l], ln2_b[l], alpha=alpha, bm=MS, bk=t["down_bk"])
        outs["ks"].append(k32.reshape(BS, NQ, HA, DK)[:, :LS])
        outs["vs"].append(v32.reshape(BS, NQ, HA, DK)[:, :LS])
        outs["Cs"].append(Cn)
        outs["ns"].append(nn)
        outs["ms"].append(mn[:, :, 0])
        outs["cs"].append(jnp.concatenate([ta[:, LS - (CONV_W - 1):LS], tg[:, LS - (CONV_W - 1):LS]], axis=-1))

    st = lambda k: jnp.stack(outs[k])
    return (xp32.reshape(B, L, D), xs32.reshape(BS, NQ, D)[:, :LS],
            st("kp"), st("vp"), st("ks"), st("vs"),
            st("Cp"), st("np"), st("mp"), st("Cs"), st("ns"), st("ms"),
            st("cp"), st("cs"))
```

```python
import functools
import math

import numpy as np
import jax
import jax.numpy as jnp
from jax import lax
from jax.experimental import pallas as pl
from jax.experimental.pallas import tpu as pltpu

F32 = jnp.float32
BF16 = jnp.bfloat16

LN_EPS = 1e-5
N_BUCKETS = 32
MAX_DISTANCE = 128
CONV_W = 3
NEG = -1e30
SAMPLE_ROWS = 8
V7X_VMEM_LIMIT = 56 * 1024 * 1024


def _bucket_thresholds():
    max_exact = N_BUCKETS // 2
    d = np.arange(0, 4 * MAX_DISTANCE)
    large = max_exact + (np.log(np.maximum(d, 1) / max_exact) / math.log(MAX_DISTANCE / max_exact)
                         * (N_BUCKETS - max_exact)).astype(np.int64)
    bucket = np.where(d < max_exact, d, np.minimum(large, N_BUCKETS - 1))
    return tuple(int(d[bucket >= k].min()) for k in range(1, N_BUCKETS))


_THR = _bucket_thresholds()
_FAR = _THR[-1]


def _cparams(sem):
    return pltpu.CompilerParams(dimension_semantics=sem, vmem_limit_bytes=V7X_VMEM_LIMIT)


def _nt(a, b):
    return lax.dot_general(a, b, (((1,), (1,)), ((), ())), preferred_element_type=F32)


def _tn(a, b):
    return lax.dot_general(a, b, (((0,), (0,)), ((), ())), preferred_element_type=F32)


def _sigmoid(x):
    return 1.0 / (1.0 + jnp.exp(-x))


def _pbias_kernel(rb_ref, o_ref, *, T):
    h = pl.program_id(0)
    r = lax.broadcasted_iota(jnp.int32, (T, T), 0)
    c = lax.broadcasted_iota(jnp.int32, (T, T), 1)
    for t, off in enumerate((0, T)):
        d = r - c + off
        bias = jnp.full((T, T), rb_ref[0, h], F32)
        for k in range(1, N_BUCKETS):
            bias = jnp.where(d >= _THR[k - 1], rb_ref[k, h], bias)
        if off == 0:
            bias = jnp.where(d >= 0, bias, NEG)
        o_ref[0, t] = bias


def _prompt_bias(rel_bias, T):
    H = rel_bias.shape[1]
    return pl.pallas_call(
        functools.partial(_pbias_kernel, T=T),
        grid=(H,),
        in_specs=[pl.BlockSpec(memory_space=pltpu.SMEM)],
        out_specs=pl.BlockSpec((1, 2, T, T), lambda h: (h, 0, 0, 0)),
        out_shape=jax.ShapeDtypeStruct((H, 2, T, T), F32),
        compiler_params=_cparams(("arbitrary",)),
        name="prompt_bias",
    )(rel_bias)


def _dbias_kernel(rb_ref, d_ref, n_ref, *, P, G):
    h = pl.program_id(0)
    NQ = SAMPLE_ROWS
    R = 2 * NQ

    def build(d):
        bias = jnp.full(d.shape, rb_ref[0, h], F32)
        for k in range(1, N_BUCKETS):
            bias = jnp.where(d >= _THR[k - 1], rb_ref[k, h], bias)
        return bias

    r = lax.broadcasted_iota(jnp.int32, (R, P), 0)
    c = lax.broadcasted_iota(jnp.int32, (R, P), 1)
    qi = r % NQ
    far = jnp.full((R, P), rb_ref[N_BUCKETS - 1, h], F32)
    last = build(P + qi - c)
    dn = qi - c
    new = jnp.where(dn >= 0, build(jnp.maximum(dn, 0)), NEG)
    d_ref[0] = jnp.concatenate([far] * G, axis=1)
    d_ref[1] = jnp.concatenate([far] * (G - 1) + [last], axis=1)
    n_ref[...] = new


def _decode_bias(rel_bias, P, G):
    H = rel_bias.shape[1]
    R = 2 * SAMPLE_ROWS
    return pl.pallas_call(
        functools.partial(_dbias_kernel, P=P, G=G),
        grid=(H,),
        in_specs=[pl.BlockSpec(memory_space=pltpu.SMEM)],
        out_specs=[pl.BlockSpec((2, R, G * P), lambda h: (0, h, 0)),
                   pl.BlockSpec((R, P), lambda h: (h, 0))],
        out_shape=(jax.ShapeDtypeStruct((2, H * R, G * P), F32), jax.ShapeDtypeStruct((H * R, P), F32)),
        compiler_params=_cparams(("arbitrary",)),
        name="decode_bias",
    )(rel_bias)


def _inproj_kernel(x_ref, w_ref, wg_ref, pb_ref, k_ref, v_ref, om_ref, g_ref, *, H, NG):
    j = pl.program_id(1)
    x = x_ref[...].astype(BF16)
    acc = jnp.dot(x, w_ref[...], preferred_element_type=F32)
    bm, W = acc.shape
    DK = W // H

    def store_heads(ref):
        for h in range(H):
            ref[pl.ds(h, bm, stride=H), :] = acc[:, h * DK:(h + 1) * DK]

    @pl.when(j <= 5)
    def _():
        pb_ref[...] = acc.astype(pb_ref.dtype)

    @pl.when(j == 1)
    def _():
        store_heads(k_ref)

    @pl.when(j == 2)
    def _():
        store_heads(v_ref)

    @pl.when(j == 6)
    def _():
        om_ref[...] = acc

    @pl.when(j == 0)
    def _():
        gates = jnp.dot(x, wg_ref[...], preferred_element_type=F32)
        if bm % 128:
            gates = jnp.concatenate([gates, jnp.zeros((128 - bm % 128, gates.shape[1]), F32)], axis=0)
        g_ref[0] = gates.T[:NG, :bm]


def _inproj(x, w, layer, wg, *, bm, rows_per_seq, pb_dtype, W, H, NG):
    M, D = x.shape
    DK = W // H
    tpb = rows_per_seq // bm
    grid = (M // bm, 7)
    return pl.pallas_call(
        functools.partial(_inproj_kernel, H=H, NG=NG),
        grid=grid,
        in_specs=[pl.BlockSpec((bm, D), lambda i, j: (i, 0)),
                  pl.BlockSpec((None, D, W), lambda i, j: (layer, 0, j)),
                  pl.BlockSpec(wg.shape, lambda i, j: (0, 0))],
        out_specs=[pl.BlockSpec((bm, W), lambda i, j: (i, jnp.minimum(j, 5))),
                   pl.BlockSpec((bm * H, DK), lambda i, j: (i, 0)),
                   pl.BlockSpec((bm * H, DK), lambda i, j: (i, 0)),
                   pl.BlockSpec((bm, W), lambda i, j: (i, 0)),
                   pl.BlockSpec((1, NG, bm), lambda i, j: (i // tpb, 0, i % tpb))],
        out_shape=[jax.ShapeDtypeStruct((M, 6 * W), pb_dtype),
                   jax.ShapeDtypeStruct((M * H, DK), F32),
                   jax.ShapeDtypeStruct((M * H, DK), F32),
                   jax.ShapeDtypeStruct((M, W), F32),
                   jax.ShapeDtypeStruct((M // rows_per_seq, NG, rows_per_seq), F32)],
        compiler_params=_cparams(("arbitrary", "arbitrary")),
        name="in_proj",
    )(x, w, wg)


def _deepnorm(res, y, g_ref, b_ref, o32_ref, o16_ref, alpha):
    z = alpha * res + y
    mu = jnp.mean(z, axis=-1, keepdims=True)
    zc = z - mu
    var = jnp.mean(zc * zc, axis=-1, keepdims=True)
    o = zc * lax.rsqrt(var + LN_EPS) * g_ref[...] + b_ref[...]
    o32_ref[...] = o
    o16_ref[...] = o.astype(BF16)


def _outproj_ln_kernel(x1_ref, x2_ref, w1_ref, w2_ref, r_ref, g_ref, b_ref, o32_ref, o16_ref, *, alpha):
    mix = (jnp.dot(x1_ref[...].astype(BF16), w1_ref[...], preferred_element_type=F32)
           + jnp.dot(x2_ref[...].astype(BF16), w2_ref[...], preferred_element_type=F32))
    _deepnorm(r_ref[...], mix, g_ref, b_ref, o32_ref, o16_ref, alpha)


def _outproj_ln(x1, x2, w, layer, res, g, b, *, alpha, bm):
    M, K1 = x1.shape
    N = w.shape[2]
    row = pl.BlockSpec((bm, N), lambda i: (i, 0))
    vec = pl.BlockSpec((1, N), lambda i: (0, 0))
    return pl.pallas_call(
        functools.partial(_outproj_ln_kernel, alpha=alpha),
        grid=(M // bm,),
        in_specs=[pl.BlockSpec((bm, K1), lambda i: (i, 0)),
                  pl.BlockSpec((bm, K1), lambda i: (i, 0)),
                  pl.BlockSpec((None, K1, N), lambda i: (layer, 0, 0)),
                  pl.BlockSpec((None, K1, N), lambda i: (layer, 1, 0)),
                  row, vec, vec],
        out_specs=[row, row],
        out_shape=[jax.ShapeDtypeStruct((M, N), F32), jax.ShapeDtypeStruct((M, N), BF16)],
        compiler_params=_cparams(("arbitrary",)),
        name="out_proj_ln",
    )(x1, x2, w, w, res, g.reshape(1, N), b.reshape(1, N))


def _down_ln_kernel(x_ref, w_ref, r_ref, g_ref, b_ref, o32_ref, o16_ref, acc_ref, *, alpha, nk):
    k = pl.program_id(1)
    d = jnp.dot(x_ref[...].astype(BF16), w_ref[...], preferred_element_type=F32)

    @pl.when(k == 0)
    def _():
        acc_ref[...] = d

    @pl.when(k > 0)
    def _():
        acc_ref[...] += d

    @pl.when(k == nk - 1)
    def _():
        _deepnorm(r_ref[...], acc_ref[...], g_ref, b_ref, o32_ref, o16_ref, alpha)


def _down_ln(x, w, layer, res, g, b, *, alpha, bm, bk):
    M, K = x.shape
    N = w.shape[2]
    nk = K // bk
    row = pl.BlockSpec((bm, N), lambda i, k: (i, 0))
    vec = pl.BlockSpec((1, N), lambda i, k: (0, 0))
    return pl.pallas_call(
        functools.partial(_down_ln_kernel, alpha=alpha, nk=nk),
        grid=(M // bm, nk),
        in_specs=[pl.BlockSpec((bm, bk), lambda i, k: (i, k)),
                  pl.BlockSpec((None, bk, N), lambda i, k: (layer, k, 0)),
                  row, vec, vec],
        out_specs=[row, row],
        out_shape=[jax.ShapeDtypeStruct((M, N), F32), jax.ShapeDtypeStruct((M, N), BF16)],
        scratch_shapes=[pltpu.VMEM((bm, N), F32)],
        compiler_params=_cparams(("arbitrary", "arbitrary")),
        name="down_proj_ln",
    )(x, w, res, g.reshape(1, N), b.reshape(1, N))


def _upconv_kernel(x_ref, wa_ref, wg_ref, cwa_ref, cwg_ref, cba_ref, cbg_ref, c0a_ref, c0g_ref,
                   act_ref, ta_ref, tg_ref, wa16_ref, wg16_ref, ha_ref, hg_ref, *, bm, cols, seq_rows):
    @pl.when((pl.program_id(1) == 0) & (pl.program_id(2) == 0))
    def _():
        wa16_ref[...] = wa_ref[...].astype(BF16)
        wg16_ref[...] = wg_ref[...].astype(BF16)

    x = x_ref[...].astype(BF16)
    fresh = seq_rows < bm
    first = pl.program_id(2) == 0
    if fresh:
        rmod = lax.broadcasted_iota(jnp.int32, (bm, cols), 0) & (seq_rows - 1)
    else:
        rid = lax.broadcasted_iota(jnp.int32, (8, cols), 0)
    for c in range(act_ref.shape[1] // cols):
        cs = slice(c * cols, (c + 1) * cols)
        conv = []
        for w_ref, cw_ref, cb_ref, c0_ref, t_ref, h_ref in ((wa16_ref, cwa_ref, cba_ref, c0a_ref, ta_ref, ha_ref),
                                                            (wg16_ref, cwg_ref, cbg_ref, c0g_ref, tg_ref, hg_ref)):
            u = jnp.dot(x, w_ref[:, cs], preferred_element_type=F32)
            u1 = pltpu.roll(u, 1, axis=0)
            u2 = pltpu.roll(u, 2, axis=0)
            if fresh:
                u1 = jnp.where(rmod == 0, c0_ref[0, :, cs], u1)
                u2 = jnp.where(rmod < 2, c0_ref[1, :, cs], u2)
                t_ref[:, cs] = u
            else:
                c0 = c0_ref[0, :, cs]
                prev = h_ref[:, cs]
                p2 = jnp.where(first, c0[0:1], prev[6:7])
                p1 = jnp.where(first, c0[1:2], prev[7:8])
                h1 = jnp.where(rid == 0, p1, u1[:8])
                h2 = jnp.where(rid == 0, p2, jnp.where(rid == 1, p1, u2[:8]))
                u1 = jnp.concatenate([h1, u1[8:]], axis=0)
                u2 = jnp.concatenate([h2, u2[8:]], axis=0)
                h_ref[:, cs] = u[bm - 8:]
                t_ref[0, :, cs] = u[bm - 8:]
            cw = cw_ref[:, cs]
            conv.append(cb_ref[:, cs] + cw[0:1] * u2 + cw[1:2] * u1 + cw[2:3] * u)

        a_br, g_br = conv
        act_ref[:, cs] = (g_br * _sigmoid(g_br) * a_br).astype(act_ref.dtype)


def _upconv(x, w_up, layer, conv_w, conv_b, conv0, *, bm, bn, seq_rows, act_dtype):
    M, D = x.shape
    F = w_up.shape[2] // 2
    NT = F // bn
    fresh = seq_rows < bm
    nb, nmt = (M // bm, 1) if fresh else (M // seq_rows, seq_rows // bm)
    cb = conv_b.reshape(1, 2 * F)
    half = lambda off: (lambda j, b, i: (0, off + j))
    wsel = lambda off: (lambda j, b, i: (layer, 0, off + j))
    rows = lambda j, b, i: (b * nmt + i, j)
    if fresh:
        c0_spec = lambda off: pl.BlockSpec((2, bm, bn), lambda j, b, i: (0, b, off + j))
        tail_spec = pl.BlockSpec((bm, bn), rows)
        tail_shape = jax.ShapeDtypeStruct((M, F), F32)
    else:
        c0_spec = lambda off: pl.BlockSpec((1, CONV_W - 1, bn), lambda j, b, i: (b, 0, off + j))
        tail_spec = pl.BlockSpec((1, 8, bn), lambda j, b, i: (b, 0, j))
        tail_shape = jax.ShapeDtypeStruct((nb, 8, F), F32)
    return pl.pallas_call(
        functools.partial(_upconv_kernel, bm=bm, cols=math.gcd(bn, 256), seq_rows=seq_rows),
        grid=(NT, nb, nmt),
        in_specs=[pl.BlockSpec((bm, D), lambda j, b, i: (b * nmt + i, 0)),
                  pl.BlockSpec((None, D, bn), wsel(0)), pl.BlockSpec((None, D, bn), wsel(NT)),
                  pl.BlockSpec((CONV_W, bn), half(0)), pl.BlockSpec((CONV_W, bn), half(NT)),
                  pl.BlockSpec((1, bn), half(0)), pl.BlockSpec((1, bn), half(NT)),
                  c0_spec(0), c0_spec(NT)],
        out_specs=[pl.BlockSpec((bm, bn), rows), tail_spec, tail_spec],
        out_shape=[jax.ShapeDtypeStruct((M, F), act_dtype), tail_shape, tail_shape],
        scratch_shapes=[pltpu.VMEM((D, bn), BF16), pltpu.VMEM((D, bn), BF16),
                        pltpu.VMEM((8, bn), F32), pltpu.VMEM((8, bn), F32)],
        compiler_params=_cparams(("arbitrary", "arbitrary", "arbitrary")),
        name="up_conv_gate",
    )(x, w_up, w_up, conv_w, conv_w, cb, cb, conv0, conv0)


def _lambda(lv_ref, lam_init):
    lv = lv_ref[...]
    s1 = jnp.sum(lv[0:1] * lv[1:2], axis=1, keepdims=True)
    s2 = jnp.sum(lv[2:3] * lv[3:4], axis=1, keepdims=True)
    return jnp.exp(s1) - jnp.exp(s2) + lam_init


def _pattn_kernel(li_ref, rb_ref, q_ref, k_ref, v_ref, pbias_ref, lv_ref, sw_ref, o_ref,
                  m_ref, l_ref, acc_ref, *, T, DH, HG):
    hg = pl.program_id(1)
    i = pl.program_id(2)
    DV = 2 * DH
    lane = lax.broadcasted_iota(jnp.int32, (T, DV), 1)
    qs = []
    for hh in range(HG):
        q = q_ref[:, hh * DV:(hh + 1) * DV].astype(F32) * (DH ** -0.5)
        qs += [jnp.where(lane < DH, q, 0.0).astype(BF16), jnp.where(lane >= DH, q, 0.0).astype(BF16)]
    m_ref[...] = jnp.full(m_ref.shape, NEG, F32)
    l_ref[...] = jnp.zeros(l_ref.shape, F32)
    acc_ref[...] = jnp.zeros(acc_ref.shape, F32)
    reps = T // DV

    def tile(j, bias_of):
        start = pl.multiple_of(j * T, T)
        for hh in range(HG):
            kt = k_ref[pl.ds(start, T), hh * DV:(hh + 1) * DV]
            vt = v_ref[pl.ds(start, T), hh * DV:(hh + 1) * DV]
            bias = bias_of(hh)
            for mp in range(2):
                c = 2 * hh + mp
                s = _nt(qs[c], kt) + bias
                m_old = m_ref[c]
                m_new = jnp.maximum(m_old, jnp.max(s, axis=1, keepdims=True))
                alpha = jnp.exp(m_old - m_new)
                p = jnp.exp(s - jnp.concatenate([m_new] * reps, axis=1))
                l_ref[c] = alpha * l_ref[c] + jnp.sum(p, axis=1, keepdims=True)
                acc_ref[c] = alpha * acc_ref[c] + jnp.dot(p.astype(BF16), vt, preferred_element_type=F32)
                m_ref[c] = m_new

    def far_body(j, carry):
        tile(j, lambda hh: rb_ref[N_BUCKETS - 1, hg * HG + hh])
        return carry

    lax.fori_loop(0, jnp.maximum(i - 1, 0), far_body, 0)

    @pl.when(i >= 1)
    def _():
        tile(i - 1, lambda hh: pbias_ref[hh, 1])

    tile(i, lambda hh: pbias_ref[hh, 0])

    lam_init = li_ref[0]
    lam = _lambda(lv_ref, lam_init)
    for hh in range(HG):
        o = acc_ref[2 * hh] / l_ref[2 * hh] - lam * (acc_ref[2 * hh + 1] / l_ref[2 * hh + 1])
        o = o * lax.rsqrt(jnp.mean(o * o, axis=-1, keepdims=True) + LN_EPS) * sw_ref[...] * (1.0 - lam_init)
        o_ref[:, hh * DV:(hh + 1) * DV] = o.astype(o_ref.dtype)


def _prompt_attention(pb, rel_bias, pbias, lam_vecs, subln_w, lam_init, *, B, L, H, DH, T, HG):
    M = pb.shape[0]
    DV = 2 * DH
    nq = L // T
    GW = HG * DV
    NG = H // HG
    stat = pltpu.VMEM((2 * HG, T, DV), F32)
    return pl.pallas_call(
        functools.partial(_pattn_kernel, T=T, DH=DH, HG=HG),
        grid=(B, NG, nq),
        in_specs=[pl.BlockSpec(memory_space=pltpu.SMEM),
                  pl.BlockSpec(memory_space=pltpu.SMEM),
                  pl.BlockSpec((T, GW), lambda b, g, i: (b * nq + i, g)),
                  pl.BlockSpec((L, GW), lambda b, g, i: (b, NG + g)),
                  pl.BlockSpec((L, GW), lambda b, g, i: (b, 2 * NG + g)),
                  pl.BlockSpec((HG, 2, T, T), lambda b, g, i: (g, 0, 0, 0)),
                  pl.BlockSpec((4, DH), lambda b, g, i: (0, 0)),
                  pl.BlockSpec((1, DV), lambda b, g, i: (0, 0))],
        out_specs=pl.BlockSpec((T, GW), lambda b, g, i: (b * nq + i, g)),
        out_shape=jax.ShapeDtypeStruct((M, H * DV), BF16),
        scratch_shapes=[stat, stat, stat],
        compiler_params=_cparams(("arbitrary", "arbitrary", "arbitrary")),
        name="prompt_attention",
    )(lam_init, rel_bias, pb, pb, pb, pbias, lam_vecs, subln_w.reshape(1, DV))


def _dattn_kernel(pt_ref, li_ref, q_ref, kn_ref, vn_ref, db_ref, nb_ref, lv_ref, sw_ref, *rest,
                  G, P, H, DH, nsteps):
    del pt_ref
    k_refs = rest[:G]
    v_refs = rest[G:2 * G]
    o_ref, qb_ref, m_ref, l_ref, acc_ref = rest[2 * G:]
    NQ = SAMPLE_ROWS
    DV = 2 * DH
    WA = H * DV
    RH = 2 * NQ
    s_id = pl.program_id(1)

    @pl.when(s_id == 0)
    def _():
        q = q_ref[...] * (DH ** -0.5)
        lane = lax.broadcasted_iota(jnp.int32, (NQ, WA), 1)
        blocks = []
        for h in range(H):
            for mp in range(2):
                lo = h * DV + mp * DH
                blocks.append(jnp.where((lane >= lo) & (lane < lo + DH), q, 0.0))
        qb_ref[...] = jnp.concatenate(blocks, axis=0).astype(BF16)
        m_ref[...] = jnp.full(m_ref.shape, NEG, F32)
        l_ref[...] = jnp.zeros(l_ref.shape, F32)
        acc_ref[...] = jnp.zeros(acc_ref.shape, F32)

    def update(kc, v_heads, bias):
        s = _nt(qb_ref[...], kc) + bias
        m_old = m_ref[...]
        m_new = jnp.maximum(m_old, jnp.max(s, axis=1, keepdims=True))
        alpha = jnp.exp(m_old - m_new)
        p = jnp.exp(s - jnp.concatenate([m_new] * (s.shape[1] // DV), axis=1))
        l_ref[...] = alpha * l_ref[...] + jnp.sum(p, axis=1, keepdims=True)
        pb = p.astype(BF16)
        pv = jnp.concatenate([jnp.dot(pb[h * RH:(h + 1) * RH], v_heads[h], preferred_element_type=F32)
                              for h in range(H)], axis=0)
        acc_ref[...] = alpha * acc_ref[...] + pv
        m_ref[...] = m_new

    def head_rows(refs, h):
        return jnp.concatenate([r[pl.ds(h, P, stride=H), :] for r in refs], axis=0).astype(BF16)

    is_last = (s_id == nsteps - 1).astype(jnp.int32)
    kc = jnp.concatenate([head_rows(k_refs, h) for h in range(H)], axis=1)
    update(kc, [head_rows(v_refs, h) for h in range(H)], db_ref[is_last])

    @pl.when(s_id == nsteps - 1)
    def _():
        lam_init = li_ref[0]
        lam = _lambda(lv_ref, lam_init)
        pad = jnp.zeros((P - NQ, WA), F32)
        kn = jnp.concatenate([kn_ref[...], pad], axis=0).astype(BF16)
        vn = jnp.concatenate([vn_ref[...], pad], axis=0).astype(BF16)
        update(kn, [vn[:, h * DV:(h + 1) * DV] for h in range(H)], nb_ref[...])
        a = acc_ref[...] / l_ref[...]
        for h in range(H):
            o = a[h * RH:h * RH + NQ] - lam * a[h * RH + NQ:(h + 1) * RH]
            o = o * lax.rsqrt(jnp.mean(o * o, axis=-1, keepdims=True) + LN_EPS) * sw_ref[...] * (1.0 - lam_init)
            o_ref[:, h * DV:(h + 1) * DV] = o


def _decode_attention(pb_s, cache_k, cache_v, page_idx, dbias, nbias, lam_vecs, subln_w, lam_init,
                      *, BS, NP, G, P, H, DH):
    NQ = SAMPLE_ROWS
    DV = 2 * DH
    WA = H * DV
    R = 2 * NQ * H
    nsteps = NP // G
    row = lambda col: pl.BlockSpec((NQ, WA), lambda b, s, pt: (b, col))
    full = lambda shape: pl.BlockSpec(shape, lambda b, s, pt: (0,) * len(shape))
    page = lambda g: pl.BlockSpec((None, P * H, DV), lambda b, s, pt: (pt[b * NP + s * G + g], 0, 0))
    grid_spec = pltpu.PrefetchScalarGridSpec(
        num_scalar_prefetch=1,
        grid=(BS, nsteps),
        in_specs=[pl.BlockSpec(memory_space=pltpu.SMEM),
                  row(0), row(1), row(2),
                  full((2, R, G * P)), full((R, P)), full((4, DH)), full((1, DV))]
                 + [page(g) for g in range(G)] + [page(g) for g in range(G)],
        out_specs=pl.BlockSpec((NQ, WA), lambda b, s, pt: (b, 0)),
        scratch_shapes=[pltpu.VMEM((R, WA), BF16), pltpu.VMEM((R, DV), F32), pltpu.VMEM((R, DV), F32),
                        pltpu.VMEM((R, DV), F32)])
    return pl.pallas_call(
        functools.partial(_dattn_kernel, G=G, P=P, H=H, DH=DH, nsteps=nsteps),
        grid_spec=grid_spec,
        out_shape=jax.ShapeDtypeStruct((BS * NQ, WA), F32),
        compiler_params=_cparams(("arbitrary", "arbitrary")),
        name="decode_attention",
    )(page_idx, lam_init, pb_s, pb_s, pb_s, dbias, nbias, lam_vecs, subln_w.reshape(1, DV),
      *([cache_k] * G), *([cache_v] * G))


def _log_sigmoid(x):
    return jnp.minimum(x, 0.0) - jnp.log1p(jnp.exp(-jnp.abs(x)))


def _mlstm_kernel(*refs, cs, rows, valid, zero_init, H, DM, nc):
    if zero_init:
        g_ref, gb_ref, q_ref, k_ref, v_ref, om_ref, bo_ref, mw_ref = refs[:8]
        rest = refs[8:]
    else:
        g_ref, gb_ref, q_ref, k_ref, v_ref, om_ref, bo_ref, mw_ref, c0_ref, n0_ref, m0_ref = refs[:11]
        rest = refs[11:]
    o_ref, co_ref, no_ref, mo_ref, c_s, n_s, m_s = rest
    c = pl.program_id(1)
    scale = DM ** -0.5

    @pl.when(c == 0)
    def _():
        if zero_init:
            c_s[...] = jnp.zeros(c_s.shape, F32)
            n_s[...] = jnp.zeros(n_s.shape, F32)
            m_s[...] = jnp.zeros(m_s.shape, F32)
        else:
            c_s[...] = c0_ref[0]
            n_s[...] = n0_ref[0]
            m_s[...] = m0_ref[0]

    g = g_ref[0] + gb_ref[...]
    li_all = g[:H]
    lf_all = _log_sigmoid(g[H:])
    if valid < cs:
        col = lax.broadcasted_iota(jnp.int32, (H, cs), 1)
        li_all = jnp.where(col < valid, li_all, NEG)
        lf_all = jnp.where(col < valid, lf_all, 0.0)
    rr = lax.broadcasted_iota(jnp.int32, (cs, cs), 0)
    cc = lax.broadcasted_iota(jnp.int32, (cs, cs), 1)
    causal = cc <= rr
    eye = cc == rr
    upper = (rr <= cc).astype(F32)
    b_all = jnp.dot(lf_all, upper, preferred_element_type=F32, precision=lax.Precision.HIGHEST)

    def pad_rows(x):
        if rows == cs:
            return x
        return jnp.concatenate([x, jnp.zeros((cs - rows, x.shape[1]), x.dtype)], axis=0)

    ones_dm = jnp.ones((cs, DM), BF16)

    def to_col(row):
        return jnp.sum(jnp.where(eye, row, 0.0), axis=1, keepdims=True)

    for h in range(H):
        sl = slice(h * DM, (h + 1) * DM)
        q = pad_rows(q_ref[:, sl])
        k = pad_rows(k_ref[:, sl])
        v = pad_rows(v_ref[:, sl])
        qb, kb, vb = q.astype(BF16), k.astype(BF16), v.astype(BF16)
        k32 = k.astype(F32)
        C = c_s[h]
        n = n_s[h:h + 1, :]
        m_prev = m_s[h:h + 1, :]
        li = li_all[h:h + 1, :]
        b_row = b_all[h:h + 1, :]
        r_row = li - b_row
        b_col = jnp.broadcast_to(to_col(b_row), (cs, DM))
        r_col = jnp.broadcast_to(to_col(r_row), (cs, DM))
        dmat = jnp.where(causal, jnp.concatenate([b_col] * (cs // DM), axis=1) + r_row, NEG)
        inter = b_col + m_prev
        mt = jnp.maximum(jnp.max(dmat, axis=1, keepdims=True), inter)
        w = (jnp.exp(dmat - jnp.concatenate([mt] * (cs // DM), axis=1)) * (_nt(qb, kb) * scale)).astype(BF16)
        a = jnp.exp(inter - mt)
        num = a * _nt(qb, C.astype(BF16)) + jnp.dot(w, vb, preferred_element_type=F32)
        qn = _nt(qb, jnp.broadcast_to(n, (DM, DM)).astype(BF16))
        den = a * qn + jnp.dot(w, ones_dm, preferred_element_type=F32)
        hh = num / jnp.maximum(jnp.abs(den), jnp.exp(-mt))

        bL = b_row[:, cs - 1:cs]
        m_new = jnp.maximum(bL + m_prev, jnp.max(bL + r_row, axis=1, keepdims=True))
        decay = jnp.exp(bL + m_prev - m_new)
        kw = k32 * jnp.exp(bL + r_col - m_new)
        c_s[h] = decay * C + scale * _tn(vb, kw.astype(BF16))
        n_s[h:h + 1, :] = decay * n + scale * jnp.sum(kw, axis=0, keepdims=True)
        m_s[h:h + 1, :] = jnp.broadcast_to(m_new, (1, DM))

        mu = jnp.mean(hh, axis=-1, keepdims=True)
        hc = hh - mu
        var = jnp.mean(hc * hc, axis=-1, keepdims=True)
        hn = hc * lax.rsqrt(var + LN_EPS) * mw_ref[:, sl]
        og = _sigmoid(pad_rows(om_ref[:, sl]) + bo_ref[:, sl])
        o_ref[:, sl] = (og * hn)[:rows].astype(o_ref.dtype)

    @pl.when(c == nc - 1)
    def _():
        co_ref[0] = c_s[...]
        no_ref[0] = n_s[...]
        mo_ref[0] = m_s[...]


def _mlstm(gates, gate_bias, pb, om32, b_o, mhn_w, state, *, nseq, cs, rows, valid, H, DM, out_dtype):
    WM = H * DM
    M = pb.shape[0]
    nc = gates.shape[2] // cs
    zero_init = state is None
    blk = lambda col: pl.BlockSpec((rows, WM), lambda b, c: (b * nc + c, col))
    vec = pl.BlockSpec((1, WM), lambda b, c: (0, 0))
    st_c = pl.BlockSpec((1, H, DM, DM), lambda b, c: (b, 0, 0, 0))
    st_v = pl.BlockSpec((1, H, DM), lambda b, c: (b, 0, 0))
    in_specs = [pl.BlockSpec((1, 2 * H, cs), lambda b, c: (b, 0, c)),
                pl.BlockSpec((2 * H, 1), lambda b, c: (0, 0)),
                blk(3), blk(4), blk(5), blk(0), vec, vec]
    args = [gates, gate_bias, pb, pb, pb, om32, b_o.reshape(1, WM), mhn_w.reshape(1, WM)]
    if not zero_init:
        in_specs += [st_c, st_v, st_v]
        args += list(state)
    return pl.pallas_call(
        functools.partial(_mlstm_kernel, cs=cs, rows=rows, valid=valid, zero_init=zero_init, H=H, DM=DM, nc=nc),
        grid=(nseq, nc),
        in_specs=in_specs,
        out_specs=[blk(0), st_c, st_v, st_v],
        out_shape=[jax.ShapeDtypeStruct((M, WM), out_dtype),
                   jax.ShapeDtypeStruct((nseq, H, DM, DM), F32),
                   jax.ShapeDtypeStruct((nseq, H, DM), F32),
                   jax.ShapeDtypeStruct((nseq, H, DM), F32)],
        scratch_shapes=[pltpu.VMEM((H, DM, DM), F32), pltpu.VMEM((H, DM), F32), pltpu.VMEM((H, DM), F32)],
        compiler_params=_cparams(("arbitrary", "arbitrary")),
        name="mlstm",
    )(*args)


def _largest_divisor(n, cap, mult):
    best = None
    for t in range(mult, min(n, cap) + 1, mult):
        if n % t == 0:
            best = t
    assert best is not None, (n, cap, mult)
    return best


def _tiles(L, D, F):
    return dict(
        attn_t=_largest_divisor(L, 512, 128),
        attn_heads=2,
        mlstm_cs=_largest_divisor(L, 256, 128),
        inproj_bm=_largest_divisor(L, 512, 8),
        out_bm=_largest_divisor(L, 512, 8),
        up_bm=_largest_divisor(L, 1024, 8),
        up_bn=_largest_divisor(F, 512, 128),
        up_bn_sample=_largest_divisor(F, 704, 128),
        down_bm=_largest_divisor(L, 512, 8),
        down_bk=_largest_divisor(F, 1408, 128),
        pages_per_step=16,
    )


def kernel(x_prompt, x_sample, cache_k, cache_v, page_table, state_C, state_n, state_m, state_conv, rel_bias, w_in, b_i, b_f, b_o, lambda_q1, lambda_k1, lambda_q2, lambda_k2, subln_w, mhn_w, w_out, ln1_g, ln1_b, w_up, conv_w, conv_b, w_down, ln2_g, ln2_b):
    B, L, D = x_prompt.shape
    BS, LS, _ = x_sample.shape
    DEPTH, NPOOL, P, HA, DK = cache_k.shape
    DH = DK // 2
    WA = HA * DK
    HM, DM = state_C.shape[2], state_C.shape[3]
    WM = HM * DM
    F = w_down.shape[1]
    NP = page_table.shape[1]
    NQ = SAMPLE_ROWS
    assert WA == WM and w_in.shape[2] == 3 * WA + 4 * WM + 2 * HM
    assert LS <= NQ and LS >= CONV_W - 1 and P >= _FAR and HA == 8 and HM == 8
    t = _tiles(L, D, F)
    T = t["attn_t"]
    assert T >= _FAR
    G = min(t["pages_per_step"], NP)
    assert NP % G == 0
    alpha = (2 * DEPTH) ** 0.25

    pbias = _prompt_bias(rel_bias, T)
    dbias, nbias = _decode_bias(rel_bias, P, G)
    cache_k2 = cache_k.reshape(DEPTH * NPOOL, P * HA, DK)
    cache_v2 = cache_v.reshape(DEPTH * NPOOL, P * HA, DK)
    zconv = jnp.zeros((B, CONV_W - 1, 2 * F), F32)
    sample_cs = 128

    xp32 = x_prompt.reshape(B * L, D)
    xs32 = jnp.pad(x_sample, ((0, 0), (0, NQ - LS), (0, 0))).reshape(BS * NQ, D)
    xp16, xs16 = xp32, xs32
    wi16 = w_in.astype(BF16)
    wo16 = w_out.astype(BF16)
    wd16 = w_down.astype(BF16)

    outs = {k: [] for k in ("kp", "vp", "ks", "vs", "Cp", "np", "mp", "Cs", "ns", "ms", "cp", "cs")}
    for l in range(DEPTH):
        lam0 = 0.8 - 0.6 * math.exp(-0.3 * l)
        lam_init = jnp.full((1,), lam0, F32)
        lam_vecs = jnp.stack([lambda_q1[l], lambda_k1[l], lambda_q2[l], lambda_k2[l]])
        wg = jnp.pad(w_in[l, :, 3 * WA + 4 * WM:], ((0, 0), (0, 128 - 2 * HM))).astype(BF16)
        gate_bias = jnp.concatenate([b_i[l], b_f[l]]).reshape(2 * HM, 1)
        page_idx = (page_table + l * NPOOL).reshape(-1).astype(jnp.int32)

        pb, k32, v32, om32, gates = _inproj(xp16, wi16, l, wg, bm=t["inproj_bm"], rows_per_seq=L, pb_dtype=BF16,
                                            W=WA, H=HA, NG=2 * HM)
        oa = _prompt_attention(pb, rel_bias, pbias, lam_vecs, subln_w[l], lam_init, B=B, L=L, H=HA, DH=DH, T=T,
                               HG=t["attn_heads"])
        om, Cn, nn, mn = _mlstm(gates, gate_bias, pb, om32, b_o[l], mhn_w[l], None, nseq=B, cs=t["mlstm_cs"],
                                rows=t["mlstm_cs"], valid=t["mlstm_cs"], H=HM, DM=DM, out_dtype=BF16)
        h32, h16 = _outproj_ln(oa, om, wo16, l, xp32, ln1_g[l], ln1_b[l], alpha=alpha, bm=t["out_bm"])
        act, ta, tg = _upconv(h16, w_up, l, conv_w[l], conv_b[l], zconv, bm=t["up_bm"], bn=t["up_bn"],
                              seq_rows=L, act_dtype=BF16)
        xp32, xp16 = _down_ln(act, wd16, l, h32, ln2_g[l], ln2_b[l], alpha=alpha, bm=t["down_bm"], bk=t["down_bk"])
        outs["kp"].append(k32.reshape(B, L, HA, DK))
        outs["vp"].append(v32.reshape(B, L, HA, DK))
        outs["Cp"].append(Cn)
        outs["np"].append(nn)
        outs["mp"].append(mn[:, :, 0])
        outs["cp"].append(jnp.concatenate([ta[:, 8 - (CONV_W - 1):], tg[:, 8 - (CONV_W - 1):]], axis=-1))

        MS = BS * NQ
        pb, k32, v32, om32, gates = _inproj(xs16, wi16, l, wg, bm=MS, rows_per_seq=MS, pb_dtype=F32,
                                            W=WA, H=HA, NG=2 * HM)
        oa = _decode_attention(pb, cache_k2, cache_v2, page_idx, dbias, nbias, lam_vecs, subln_w[l], lam_init,
                               BS=BS, NP=NP, G=G, P=P, H=HA, DH=DH)
        gs = gates.reshape(2 * HM, BS, NQ).transpose(1, 0, 2)
        gs = jnp.pad(gs, ((0, 0), (0, 0), (0, sample_cs - NQ)))
        state = (state_C[l], state_n[l], jnp.broadcast_to(state_m[l][:, :, None], (BS, HM, DM)))
        om, Cn, nn, mn = _mlstm(gs, gate_bias, pb, om32, b_o[l], mhn_w[l], state, nseq=BS, cs=sample_cs,
                                rows=NQ, valid=LS, H=HM, DM=DM, out_dtype=F32)
        h32, h16 = _outproj_ln(oa, om, wo16, l, xs32, ln1_g[l], ln1_b[l], alpha=alpha, bm=MS)
        sc = state_conv[l]
        c0x = jnp.stack([jnp.pad(sc[:, CONV_W - 2:], ((0, 0), (0, NQ - 1), (0, 0))),
                         jnp.pad(sc, ((0, 0), (0, NQ - (CONV_W - 1)), (0, 0)))]).reshape(2, MS, 2 * F)
        act, ta, tg = _upconv(h16, w_up, l, conv_w[l], conv_b[l], c0x, bm=MS, bn=t["up_bn_sample"],
                              seq_rows=NQ, act_dtype=F32)
        ta, tg = ta.reshape(BS, NQ, F), tg.reshape(BS, NQ, F)
        xs32, xs16 = _down_ln(act, wd16, l, h32, ln2_g[l], ln2_b[l], alpha=alpha, bm=MS, bk=t["down_bk"])
        outs["ks"].append(k32.reshape(BS, NQ, HA, DK)[:, :LS])
        outs["vs"].append(v32.reshape(BS, NQ, HA, DK)[:, :LS])
        outs["Cs"].append(Cn)
        outs["ns"].append(nn)
        outs["ms"].append(mn[:, :, 0])
        outs["cs"].append(jnp.concatenate([ta[:, LS - (CONV_W - 1):LS], tg[:, LS - (CONV_W - 1):LS]], axis=-1))

    st = lambda k: jnp.stack(outs[k])
    return (xp32.reshape(B, L, D), xs32.reshape(BS, NQ, D)[:, :LS],
            st("kp"), st("vp"), st("ks"), st("vs"),
            st("Cp"), st("np"), st("mp"), st("Cs"), st("ns"), st("ms"),
            st("cp"), st("cs"))
```

```python
import functools
import math

import numpy as np
import jax
import jax.numpy as jnp
from jax import lax
from jax.experimental import pallas as pl
from jax.experimental.pallas import tpu as pltpu

F32 = jnp.float32
BF16 = jnp.bfloat16

LN_EPS = 1e-5
N_BUCKETS = 32
MAX_DISTANCE = 128
CONV_W = 3
NEG = -1e30
SAMPLE_ROWS = 8
V7X_VMEM_LIMIT = 56 * 1024 * 1024


def _bucket_thresholds():
    max_exact = N_BUCKETS // 2
    d = np.arange(0, 4 * MAX_DISTANCE)
    large = max_exact + (np.log(np.maximum(d, 1) / max_exact) / math.log(MAX_DISTANCE / max_exact)
                         * (N_BUCKETS - max_exact)).astype(np.int64)
    bucket = np.where(d < max_exact, d, np.minimum(large, N_BUCKETS - 1))
    return tuple(int(d[bucket >= k].min()) for k in range(1, N_BUCKETS))


_THR = _bucket_thresholds()
_FAR = _THR[-1]


def _cparams(sem):
    return pltpu.CompilerParams(dimension_semantics=sem, vmem_limit_bytes=V7X_VMEM_LIMIT)


def _nt(a, b):
    return lax.dot_general(a, b, (((1,), (1,)), ((), ())), preferred_element_type=F32)


def _tn(a, b):
    return lax.dot_general(a, b, (((0,), (0,)), ((), ())), preferred_element_type=F32)


def _sigmoid(x):
    return 1.0 / (1.0 + jnp.exp(-x))


def _pbias_kernel(rb_ref, o_ref, *, T):
    h = pl.program_id(0)
    r = lax.broadcasted_iota(jnp.int32, (T, T), 0)
    c = lax.broadcasted_iota(jnp.int32, (T, T), 1)
    for t, off in enumerate((0, T)):
        d = r - c + off
        bias = jnp.full((T, T), rb_ref[0, h], F32)
        for k in range(1, N_BUCKETS):
            bias = jnp.where(d >= _THR[k - 1], rb_ref[k, h], bias)
        if off == 0:
            bias = jnp.where(d >= 0, bias, NEG)
        o_ref[0, t] = bias


def _prompt_bias(rel_bias, T):
    H = rel_bias.shape[1]
    return pl.pallas_call(
        functools.partial(_pbias_kernel, T=T),
        grid=(H,),
        in_specs=[pl.BlockSpec(memory_space=pltpu.SMEM)],
        out_specs=pl.BlockSpec((1, 2, T, T), lambda h: (h, 0, 0, 0)),
        out_shape=jax.ShapeDtypeStruct((H, 2, T, T), F32),
        compiler_params=_cparams(("arbitrary",)),
        name="prompt_bias",
    )(rel_bias)


def _dbias_kernel(rb_ref, d_ref, n_ref, *, P, G):
    h = pl.program_id(0)
    NQ = SAMPLE_ROWS
    R = 2 * NQ

    def build(d):
        bias = jnp.full(d.shape, rb_ref[0, h], F32)
        for k in range(1, N_BUCKETS):
            bias = jnp.where(d >= _THR[k - 1], rb_ref[k, h], bias)
        return bias

    r = lax.broadcasted_iota(jnp.int32, (R, P), 0)
    c = lax.broadcasted_iota(jnp.int32, (R, P), 1)
    qi = r % NQ
    far = jnp.full((R, P), rb_ref[N_BUCKETS - 1, h], F32)
    last = build(P + qi - c)
    dn = qi - c
    new = jnp.where(dn >= 0, build(jnp.maximum(dn, 0)), NEG)
    d_ref[0] = jnp.concatenate([far] * G, axis=1)
    d_ref[1] = jnp.concatenate([far] * (G - 1) + [last], axis=1)
    n_ref[...] = new


def _decode_bias(rel_bias, P, G):
    H = rel_bias.shape[1]
    R = 2 * SAMPLE_ROWS
    return pl.pallas_call(
        functools.partial(_dbias_kernel, P=P, G=G),
        grid=(H,),
        in_specs=[pl.BlockSpec(memory_space=pltpu.SMEM)],
        out_specs=[pl.BlockSpec((2, R, G * P), lambda h: (0, h, 0)),
                   pl.BlockSpec((R, P), lambda h: (h, 0))],
        out_shape=(jax.ShapeDtypeStruct((2, H * R, G * P), F32), jax.ShapeDtypeStruct((H * R, P), F32)),
        compiler_params=_cparams(("arbitrary",)),
        name="decode_bias",
    )(rel_bias)


def _inproj_kernel(x_ref, w_ref, wg_ref, pb_ref, k_ref, v_ref, om_ref, g_ref, *, H, NG):
    j = pl.program_id(1)
    x = x_ref[...].astype(BF16)
    acc = jnp.dot(x, w_ref[...], preferred_element_type=F32)
    bm, W = acc.shape
    DK = W // H

    def store_heads(ref):
        for h in range(H):
            ref[pl.ds(h, bm, stride=H), :] = acc[:, h * DK:(h + 1) * DK]

    @pl.when(j <= 5)
    def _():
        pb_ref[...] = acc.astype(pb_ref.dtype)

    @pl.when(j == 1)
    def _():
        store_heads(k_ref)

    @pl.when(j == 2)
    def _():
        store_heads(v_ref)

    @pl.when(j == 6)
    def _():
        om_ref[...] = acc

    @pl.when(j == 0)
    def _():
        gates = jnp.dot(x, wg_ref[...], preferred_element_type=F32)
        if bm % 128:
            gates = jnp.concatenate([gates, jnp.zeros((128 - bm % 128, gates.shape[1]), F32)], axis=0)
        g_ref[0] = gates.T[:NG, :bm]


def _inproj(x, w, layer, wg, *, bm, rows_per_seq, pb_dtype, W, H, NG):
    M, D = x.shape
    DK = W // H
    tpb = rows_per_seq // bm
    grid = (M // bm, 7)
    return pl.pallas_call(
        functools.partial(_inproj_kernel, H=H, NG=NG),
        grid=grid,
        in_specs=[pl.BlockSpec((bm, D), lambda i, j: (i, 0)),
                  pl.BlockSpec((None, D, W), lambda i, j: (layer, 0, j)),
                  pl.BlockSpec(wg.shape, lambda i, j: (0, 0))],
        out_specs=[pl.BlockSpec((bm, W), lambda i, j: (i, jnp.minimum(j, 5))),
                   pl.BlockSpec((bm * H, DK), lambda i, j: (i, 0)),
                   pl.BlockSpec((bm * H, DK), lambda i, j: (i, 0)),
                   pl.BlockSpec((bm, W), lambda i, j: (i, 0)),
                   pl.BlockSpec((1, NG, bm), lambda i, j: (i // tpb, 0, i % tpb))],
        out_shape=[jax.ShapeDtypeStruct((M, 6 * W), pb_dtype),
                   jax.ShapeDtypeStruct((M * H, DK), F32),
                   jax.ShapeDtypeStruct((M * H, DK), F32),
                   jax.ShapeDtypeStruct((M, W), F32),
                   jax.ShapeDtypeStruct((M // rows_per_seq, NG, rows_per_seq), F32)],
        compiler_params=_cparams(("arbitrary", "arbitrary")),
        name="in_proj",
    )(x, w, wg)


def _deepnorm(res, y, g_ref, b_ref, o32_ref, o16_ref, alpha):
    z = alpha * res + y
    mu = jnp.mean(z, axis=-1, keepdims=True)
    zc = z - mu
    var = jnp.mean(zc * zc, axis=-1, keepdims=True)
    o = zc * lax.rsqrt(var + LN_EPS) * g_ref[...] + b_ref[...]
    o32_ref[...] = o
    o16_ref[...] = o.astype(BF16)


def _outproj_ln_kernel(x1_ref, x2_ref, w1_ref, w2_ref, r_ref, g_ref, b_ref, o32_ref, o16_ref, *, alpha):
    mix = (jnp.dot(x1_ref[...].astype(BF16), w1_ref[...], preferred_element_type=F32)
           + jnp.dot(x2_ref[...].astype(BF16), w2_ref[...], preferred_element_type=F32))
    _deepnorm(r_ref[...], mix, g_ref, b_ref, o32_ref, o16_ref, alpha)


def _outproj_ln(x1, x2, w, layer, res, g, b, *, alpha, bm):
    M, K1 = x1.shape
    N = w.shape[2]
    row = pl.BlockSpec((bm, N), lambda i: (i, 0))
    vec = pl.BlockSpec((1, N), lambda i: (0, 0))
    return pl.pallas_call(
        functools.partial(_outproj_ln_kernel, alpha=alpha),
        grid=(M // bm,),
        in_specs=[pl.BlockSpec((bm, K1), lambda i: (i, 0)),
                  pl.BlockSpec((bm, K1), lambda i: (i, 0)),
                  pl.BlockSpec((None, K1, N), lambda i: (layer, 0, 0)),
                  pl.BlockSpec((None, K1, N), lambda i: (layer, 1, 0)),
                  row, vec, vec],
        out_specs=[row, row],
        out_shape=[jax.ShapeDtypeStruct((M, N), F32), jax.ShapeDtypeStruct((M, N), BF16)],
        compiler_params=_cparams(("arbitrary",)),
        name="out_proj_ln",
    )(x1, x2, w, w, res, g.reshape(1, N), b.reshape(1, N))


def _down_ln_kernel(x_ref, w_ref, r_ref, g_ref, b_ref, o32_ref, o16_ref, acc_ref, *, alpha, nk):
    k = pl.program_id(1)
    d = jnp.dot(x_ref[...].astype(BF16), w_ref[...], preferred_element_type=F32)

    @pl.when(k == 0)
    def _():
        acc_ref[...] = d

    @pl.when(k > 0)
    def _():
        acc_ref[...] += d

    @pl.when(k == nk - 1)
    def _():
        _deepnorm(r_ref[...], acc_ref[...], g_ref, b_ref, o32_ref, o16_ref, alpha)


def _down_ln(x, w, layer, res, g, b, *, alpha, bm, bk):
    M, K = x.shape
    N = w.shape[2]
    nk = K // bk
    row = pl.BlockSpec((bm, N), lambda i, k: (i, 0))
    vec = pl.BlockSpec((1, N), lambda i, k: (0, 0))
    return pl.pallas_call(
        functools.partial(_down_ln_kernel, alpha=alpha, nk=nk),
        grid=(M // bm, nk),
        in_specs=[pl.BlockSpec((bm, bk), lambda i, k: (i, k)),
                  pl.BlockSpec((None, bk, N), lambda i, k: (layer, k, 0)),
                  row, vec, vec],
        out_specs=[row, row],
        out_shape=[jax.ShapeDtypeStruct((M, N), F32), jax.ShapeDtypeStruct((M, N), BF16)],
        scratch_shapes=[pltpu.VMEM((bm, N), F32)],
        compiler_params=_cparams(("arbitrary", "arbitrary")),
        name="down_proj_ln",
    )(x, w, res, g.reshape(1, N), b.reshape(1, N))


def _upconv_kernel(x_ref, wa_ref, wg_ref, cwa_ref, cwg_ref, cba_ref, cbg_ref, c0a_ref, c0g_ref,
                   act_ref, ta_ref, tg_ref, wa16_ref, wg16_ref, ha_ref, hg_ref, *, bm, cols, seq_rows):
    @pl.when((pl.program_id(1) == 0) & (pl.program_id(2) == 0))
    def _():
        wa16_ref[...] = wa_ref[...].astype(BF16)
        wg16_ref[...] = wg_ref[...].astype(BF16)

    x = x_ref[...].astype(BF16)
    fresh = seq_rows < bm
    first = pl.program_id(2) == 0
    if fresh:
        rmod = lax.broadcasted_iota(jnp.int32, (bm, cols), 0) & (seq_rows - 1)
    else:
        rid = lax.broadcasted_iota(jnp.int32, (8, cols), 0)
    for c in range(act_ref.shape[1] // cols):
        cs = slice(c * cols, (c + 1) * cols)
        conv = []
        for w_ref, cw_ref, cb_ref, c0_ref, t_ref, h_ref in ((wa16_ref, cwa_ref, cba_ref, c0a_ref, ta_ref, ha_ref),
                                                            (wg16_ref, cwg_ref, cbg_ref, c0g_ref, tg_ref, hg_ref)):
            u = jnp.dot(x, w_ref[:, cs], preferred_element_type=F32)
            u1 = pltpu.roll(u, 1, axis=0)
            u2 = pltpu.roll(u, 2, axis=0)
            if fresh:
                u1 = jnp.where(rmod == 0, c0_ref[0, :, cs], u1)
                u2 = jnp.where(rmod < 2, c0_ref[1, :, cs], u2)
                t_ref[:, cs] = u
            else:
                c0 = c0_ref[0, :, cs]
                prev = h_ref[:, cs]
                p2 = jnp.where(first, c0[0:1], prev[6:7])
                p1 = jnp.where(first, c0[1:2], prev[7:8])
                h1 = jnp.where(rid == 0, p1, u1[:8])
                h2 = jnp.where(rid == 0, p2, jnp.where(rid == 1, p1, u2[:8]))
                u1 = jnp.concatenate([h1, u1[8:]], axis=0)
                u2 = jnp.concatenate([h2, u2[8:]], axis=0)
                h_ref[:, cs] = u[bm - 8:]
                t_ref[0, :, cs] = u[bm - 8:]
            cw = cw_ref[:, cs]
            conv.append(cb_ref[:, cs] + cw[0:1] * u2 + cw[1:2] * u1 + cw[2:3] * u)

        a_br, g_br = conv
        act_ref[:, cs] = (g_br * _sigmoid(g_br) * a_br).astype(act_ref.dtype)


def _upconv(x, w_up, layer, conv_w, conv_b, conv0, *, bm, bn, seq_rows, act_dtype):
    M, D = x.shape
    F = w_up.shape[2] // 2
    NT = F // bn
    fresh = seq_rows < bm
    nb, nmt = (M // bm, 1) if fresh else (M // seq_rows, seq_rows // bm)
    cb = conv_b.reshape(1, 2 * F)
    half = lambda off: (lambda j, b, i: (0, off + j))
    wsel = lambda off: (lambda j, b, i: (layer, 0, off + j))
    rows = lambda j, b, i: (b * nmt + i, j)
    if fresh:
        c0_spec = lambda off: pl.BlockSpec((2, bm, bn), lambda j, b, i: (0, b, off + j))
        tail_spec = pl.BlockSpec((bm, bn), rows)
        tail_shape = jax.ShapeDtypeStruct((M, F), F32)
    else:
        c0_spec = lambda off: pl.BlockSpec((1, CONV_W - 1, bn), lambda j, b, i: (b, 0, off + j))
        tail_spec = pl.BlockSpec((1, 8, bn), lambda j, b, i: (b, 0, j))
        tail_shape = jax.ShapeDtypeStruct((nb, 8, F), F32)
    return pl.pallas_call(
        functools.partial(_upconv_kernel, bm=bm, cols=math.gcd(bn, 256), seq_rows=seq_rows),
        grid=(NT, nb, nmt),
        in_specs=[pl.BlockSpec((bm, D), lambda j, b, i: (b * nmt + i, 0)),
                  pl.BlockSpec((None, D, bn), wsel(0)), pl.BlockSpec((None, D, bn), wsel(NT)),
                  pl.BlockSpec((CONV_W, bn), half(0)), pl.BlockSpec((CONV_W, bn), half(NT)),
                  pl.BlockSpec((1, bn), half(0)), pl.BlockSpec((1, bn), half(NT)),
                  c0_spec(0), c0_spec(NT)],
        out_specs=[pl.BlockSpec((bm, bn), rows), tail_spec, tail_spec],
        out_shape=[jax.ShapeDtypeStruct((M, F), act_dtype), tail_shape, tail_shape],
        scratch_shapes=[pltpu.VMEM((D, bn), BF16), pltpu.VMEM((D, bn), BF16),
                        pltpu.VMEM((8, bn), F32), pltpu.VMEM((8, bn), F32)],
        compiler_params=_cparams(("arbitrary", "arbitrary", "arbitrary")),
        name="up_conv_gate",
    )(x, w_up, w_up, conv_w, conv_w, cb, cb, conv0, conv0)


def _lambda(lv_ref, lam_init):
    lv = lv_ref[...]
    s1 = jnp.sum(lv[0:1] * lv[1:2], axis=1, keepdims=True)
    s2 = jnp.sum(lv[2:3] * lv[3:4], axis=1, keepdims=True)
    return jnp.exp(s1) - jnp.exp(s2) + lam_init


def _pattn_kernel(li_ref, rb_ref, q_ref, k_ref, v_ref, pbias_ref, lv_ref, sw_ref, o_ref,
                  m_ref, l_ref, acc_ref, *, T, DH, HG):
    hg = pl.program_id(1)
    i = pl.program_id(2)
    DV = 2 * DH
    lane = lax.broadcasted_iota(jnp.int32, (T, DV), 1)
    qs = []
    for hh in range(HG):
        q = q_ref[:, hh * DV:(hh + 1) * DV].astype(F32) * (DH ** -0.5)
        qs += [jnp.where(lane < DH, q, 0.0).astype(BF16), jnp.where(lane >= DH, q, 0.0).astype(BF16)]
    m_ref[...] = jnp.full(m_ref.shape, NEG, F32)
    l_ref[...] = jnp.zeros(l_ref.shape, F32)
    acc_ref[...] = jnp.zeros(acc_ref.shape, F32)
    reps = T // DV

    def tile(j, bias_of):
        start = pl.multiple_of(j * T, T)
        for hh in range(HG):
            kt = k_ref[pl.ds(start, T), hh * DV:(hh + 1) * DV]
            vt = v_ref[pl.ds(start, T), hh * DV:(hh + 1) * DV]
            bias = bias_of(hh)
            for mp in range(2):
                c = 2 * hh + mp
                s = _nt(qs[c], kt) + bias
                m_old = m_ref[c]
                m_new = jnp.maximum(m_old, jnp.max(s, axis=1, keepdims=True))
                alpha = jnp.exp(m_old - m_new)
                p = jnp.exp(s - jnp.concatenate([m_new] * reps, axis=1))
                l_ref[c] = alpha * l_ref[c] + jnp.sum(p, axis=1, keepdims=True)
                acc_ref[c] = alpha * acc_ref[c] + jnp.dot(p.astype(BF16), vt, preferred_element_type=F32)
                m_ref[c] = m_new

    def far_body(j, carry):
        tile(j, lambda hh: rb_ref[N_BUCKETS - 1, hg * HG + hh])
        return carry

    lax.fori_loop(0, jnp.maximum(i - 1, 0), far_body, 0)

    @pl.when(i >= 1)
    def _():
        tile(i - 1, lambda hh: pbias_ref[hh, 1])

    tile(i, lambda hh: pbias_ref[hh, 0])

    lam_init = li_ref[0]
    lam = _lambda(lv_ref, lam_init)
    for hh in range(HG):
        o = acc_ref[2 * hh] / l_ref[2 * hh] - lam * (acc_ref[2 * hh + 1] / l_ref[2 * hh + 1])
        o = o * lax.rsqrt(jnp.mean(o * o, axis=-1, keepdims=True) + LN_EPS) * sw_ref[...] * (1.0 - lam_init)
        o_ref[:, hh * DV:(hh + 1) * DV] = o.astype(o_ref.dtype)


def _prompt_attention(pb, rel_bias, pbias, lam_vecs, subln_w, lam_init, *, B, L, H, DH, T, HG):
    M = pb.shape[0]
    DV = 2 * DH
    nq = L // T
    GW = HG * DV
    NG = H // HG
    stat = pltpu.VMEM((2 * HG, T, DV), F32)
    return pl.pallas_call(
        functools.partial(_pattn_kernel, T=T, DH=DH, HG=HG),
        grid=(B, NG, nq),
        in_specs=[pl.BlockSpec(memory_space=pltpu.SMEM),
                  pl.BlockSpec(memory_space=pltpu.SMEM),
                  pl.BlockSpec((T, GW), lambda b, g, i: (b * nq + i, g)),
                  pl.BlockSpec((L, GW), lambda b, g, i: (b, NG + g)),
                  pl.BlockSpec((L, GW), lambda b, g, i: (b, 2 * NG + g)),
                  pl.BlockSpec((HG, 2, T, T), lambda b, g, i: (g, 0, 0, 0)),
                  pl.BlockSpec((4, DH), lambda b, g, i: (0, 0)),
                  pl.BlockSpec((1, DV), lambda b, g, i: (0, 0))],
        out_specs=pl.BlockSpec((T, GW), lambda b, g, i: (b * nq + i, g)),
        out_shape=jax.ShapeDtypeStruct((M, H * DV), BF16),
        scratch_shapes=[stat, stat, stat],
        compiler_params=_cparams(("arbitrary", "arbitrary", "arbitrary")),
        name="prompt_attention",
    )(lam_init, rel_bias, pb, pb, pb, pbias, lam_vecs, subln_w.reshape(1, DV))


def _dattn_kernel(pt_ref, li_ref, q_ref, kn_ref, vn_ref, db_ref, nb_ref, lv_ref, sw_ref, *rest,
                  G, P, H, DH, nsteps):
    del pt_ref
    k_refs = rest[:G]
    v_refs = rest[G:2 * G]
    o_ref, qb_ref, m_ref, l_ref, acc_ref = rest[2 * G:]
    NQ = SAMPLE_ROWS
    DV = 2 * DH
    WA = H * DV
    RH = 2 * NQ
    s_id = pl.program_id(1)

    @pl.when(s_id == 0)
    def _():
        q = q_ref[...] * (DH ** -0.5)
        lane = lax.broadcasted_iota(jnp.int32, (NQ, WA), 1)
        blocks = []
        for h in range(H):
            for mp in range(2):
                lo = h * DV + mp * DH
                blocks.append(jnp.where((lane >= lo) & (lane < lo + DH), q, 0.0))
        qb_ref[...] = jnp.concatenate(blocks, axis=0).astype(BF16)
        m_ref[...] = jnp.full(m_ref.shape, NEG, F32)
        l_ref[...] = jnp.zeros(l_ref.shape, F32)
        acc_ref[...] = jnp.zeros(acc_ref.shape, F32)

    def update(kc, v_heads, bias):
        s = _nt(qb_ref[...], kc) + bias
        m_old = m_ref[...]
        m_new = jnp.maximum(m_old, jnp.max(s, axis=1, keepdims=True))
        alpha = jnp.exp(m_old - m_new)
        p = jnp.exp(s - jnp.concatenate([m_new] * (s.shape[1] // DV), axis=1))
        l_ref[...] = alpha * l_ref[...] + jnp.sum(p, axis=1, keepdims=True)
        pb = p.astype(BF16)
        pv = jnp.concatenate([jnp.dot(pb[h * RH:(h + 1) * RH], v_heads[h], preferred_element_type=F32)
                              for h in range(H)], axis=0)
        acc_ref[...] = alpha * acc_ref[...] + pv
        m_ref[...] = m_new

    def head_rows(refs, h):
        return jnp.concatenate([r[pl.ds(h, P, stride=H), :] for r in refs], axis=0).astype(BF16)

    is_last = (s_id == nsteps - 1).astype(jnp.int32)
    kc = jnp.concatenate([head_rows(k_refs, h) for h in range(H)], axis=1)
    update(kc, [head_rows(v_refs, h) for h in range(H)], db_ref[is_last])

    @pl.when(s_id == nsteps - 1)
    def _():
        lam_init = li_ref[0]
        lam = _lambda(lv_ref, lam_init)
        pad = jnp.zeros((P - NQ, WA), F32)
        kn = jnp.concatenate([kn_ref[...], pad], axis=0).astype(BF16)
        vn = jnp.concatenate([vn_ref[...], pad], axis=0).astype(BF16)
        update(kn, [vn[:, h * DV:(h + 1) * DV] for h in range(H)], nb_ref[...])
        a = acc_ref[...] / l_ref[...]
        for h in range(H):
            o = a[h * RH:h * RH + NQ] - lam * a[h * RH + NQ:(h + 1) * RH]
            o = o * lax.rsqrt(jnp.mean(o * o, axis=-1, keepdims=True) + LN_EPS) * sw_ref[...] * (1.0 - lam_init)
            o_ref[:, h * DV:(h + 1) * DV] = o


def _decode_attention(pb_s, cache_k, cache_v, page_idx, dbias, nbias, lam_vecs, subln_w, lam_init,
                      *, BS, NP, G, P, H, DH):
    NQ = SAMPLE_ROWS
    DV = 2 * DH
    WA = H * DV
    R = 2 * NQ * H
    nsteps = NP // G
    row = lambda col: pl.BlockSpec((NQ, WA), lambda b, s, pt: (b, col))
    full = lambda shape: pl.BlockSpec(shape, lambda b, s, pt: (0,) * len(shape))
    page = lambda g: pl.BlockSpec((None, P * H, DV), lambda b, s, pt: (pt[b * NP + s * G + g], 0, 0))
    grid_spec = pltpu.PrefetchScalarGridSpec(
        num_scalar_prefetch=1,
        grid=(BS, nsteps),
        in_specs=[pl.BlockSpec(memory_space=pltpu.SMEM),
                  row(0), row(1), row(2),
                  full((2, R, G * P)), full((R, P)), full((4, DH)), full((1, DV))]
                 + [page(g) for g in range(G)] + [page(g) for g in range(G)],
        out_specs=pl.BlockSpec((NQ, WA), lambda b, s, pt: (b, 0)),
        scratch_shapes=[pltpu.VMEM((R, WA), BF16), pltpu.VMEM((R, DV), F32), pltpu.VMEM((R, DV), F32),
                        pltpu.VMEM((R, DV), F32)])
    return pl.pallas_call(
        functools.partial(_dattn_kernel, G=G, P=P, H=H, DH=DH, nsteps=nsteps),
        grid_spec=grid_spec,
        out_shape=jax.ShapeDtypeStruct((BS * NQ, WA), F32),
        compiler_params=_cparams(("arbitrary", "arbitrary")),
        name="decode_attention",
    )(page_idx, lam_init, pb_s, pb_s, pb_s, dbias, nbias, lam_vecs, subln_w.reshape(1, DV),
      *([cache_k] * G), *([cache_v] * G))


def _log_sigmoid(x):
    return jnp.minimum(x, 0.0) - jnp.log1p(jnp.exp(-jnp.abs(x)))


def _mlstm_kernel(*refs, cs, rows, valid, zero_init, H, DM, nc):
    if zero_init:
        g_ref, gb_ref, q_ref, k_ref, v_ref, om_ref, bo_ref, mw_ref = refs[:8]
        rest = refs[8:]
    else:
        g_ref, gb_ref, q_ref, k_ref, v_ref, om_ref, bo_ref, mw_ref, c0_ref, n0_ref, m0_ref = refs[:11]
        rest = refs[11:]
    o_ref, co_ref, no_ref, mo_ref, c_s, n_s, m_s = rest
    c = pl.program_id(1)
    scale = DM ** -0.5

    @pl.when(c == 0)
    def _():
        if zero_init:
            c_s[...] = jnp.zeros(c_s.shape, F32)
            n_s[...] = jnp.zeros(n_s.shape, F32)
            m_s[...] = jnp.zeros(m_s.shape, F32)
        else:
            c_s[...] = c0_ref[0]
            n_s[...] = n0_ref[0]
            m_s[...] = m0_ref[0]

    g = g_ref[0] + gb_ref[...]
    li_all = g[:H]
    lf_all = _log_sigmoid(g[H:])
    if valid < cs:
        col = lax.broadcasted_iota(jnp.int32, (H, cs), 1)
        li_all = jnp.where(col < valid, li_all, NEG)
        lf_all = jnp.where(col < valid, lf_all, 0.0)
    rr = lax.broadcasted_iota(jnp.int32, (cs, cs), 0)
    cc = lax.broadcasted_iota(jnp.int32, (cs, cs), 1)
    causal = cc <= rr
    eye = cc == rr
    upper = (rr <= cc).astype(F32)
    b_all = jnp.dot(lf_all, upper, preferred_element_type=F32, precision=lax.Precision.HIGHEST)

    def pad_rows(x):
        if rows == cs:
            return x
        return jnp.concatenate([x, jnp.zeros((cs - rows, x.shape[1]), x.dtype)], axis=0)

    ones_dm = jnp.ones((cs, DM), BF16)

    def to_col(row):
        return jnp.sum(jnp.where(eye, row, 0.0), axis=1, keepdims=True)

    for h in range(H):
        sl = slice(h * DM, (h + 1) * DM)
        q = pad_rows(q_ref[:, sl])
        k = pad_rows(k_ref[:, sl])
        v = pad_rows(v_ref[:, sl])
        qb, kb, vb = q.astype(BF16), k.astype(BF16), v.astype(BF16)
        k32 = k.astype(F32)
        C = c_s[h]
        n = n_s[h:h + 1, :]
        m_prev = m_s[h:h + 1, :]
        li = li_all[h:h + 1, :]
        b_row = b_all[h:h + 1, :]
        r_row = li - b_row
        b_col = jnp.broadcast_to(to_col(b_row), (cs, DM))
        r_col = jnp.broadcast_to(to_col(r_row), (cs, DM))
        dmat = jnp.where(causal, jnp.concatenate([b_col] * (cs // DM), axis=1) + r_row, NEG)
        inter = b_col + m_prev
        mt = jnp.maximum(jnp.max(dmat, axis=1, keepdims=True), inter)
        w = (jnp.exp(dmat - jnp.concatenate([mt] * (cs // DM), axis=1)) * (_nt(qb, kb) * scale)).astype(BF16)
        a = jnp.exp(inter - mt)
        num = a * _nt(qb, C.astype(BF16)) + jnp.dot(w, vb, preferred_element_type=F32)
        qn = _nt(qb, jnp.broadcast_to(n, (DM, DM)).astype(BF16))
        den = a * qn + jnp.dot(w, ones_dm, preferred_element_type=F32)
        hh = num / jnp.maximum(jnp.abs(den), jnp.exp(-mt))

        bL = b_row[:, cs - 1:cs]
        m_new = jnp.maximum(bL + m_prev, jnp.max(bL + r_row, axis=1, keepdims=True))
        decay = jnp.exp(bL + m_prev - m_new)
        kw = k32 * jnp.exp(bL + r_col - m_new)
        c_s[h] = decay * C + scale * _tn(vb, kw.astype(BF16))
        n_s[h:h + 1, :] = decay * n + scale * jnp.sum(kw, axis=0, keepdims=True)
        m_s[h:h + 1, :] = jnp.broadcast_to(m_new, (1, DM))

        mu = jnp.mean(hh, axis=-1, keepdims=True)
        hc = hh - mu
        var = jnp.mean(hc * hc, axis=-1, keepdims=True)
        hn = hc * lax.rsqrt(var + LN_EPS) * mw_ref[:, sl]
        og = _sigmoid(pad_rows(om_ref[:, sl]) + bo_ref[:, sl])
        o_ref[:, sl] = (og * hn)[:rows].astype(o_ref.dtype)

    @pl.when(c == nc - 1)
    def _():
        co_ref[0] = c_s[...]
        no_ref[0] = n_s[...]
        mo_ref[0] = m_s[...]


def _mlstm(gates, gate_bias, pb, om32, b_o, mhn_w, state, *, nseq, cs, rows, valid, H, DM, out_dtype):
    WM = H * DM
    M = pb.shape[0]
    nc = gates.shape[2] // cs
    zero_init = state is None
    blk = lambda col: pl.BlockSpec((rows, WM), lambda b, c: (b * nc + c, col))
    vec = pl.BlockSpec((1, WM), lambda b, c: (0, 0))
    st_c = pl.BlockSpec((1, H, DM, DM), lambda b, c: (b, 0, 0, 0))
    st_v = pl.BlockSpec((1, H, DM), lambda b, c: (b, 0, 0))
    in_specs = [pl.BlockSpec((1, 2 * H, cs), lambda b, c: (b, 0, c)),
                pl.BlockSpec((2 * H, 1), lambda b, c: (0, 0)),
                blk(3), blk(4), blk(5), blk(0), vec, vec]
    args = [gates, gate_bias, pb, pb, pb, om32, b_o.reshape(1, WM), mhn_w.reshape(1, WM)]
    if not zero_init:
        in_specs += [st_c, st_v, st_v]
        args += list(state)
    return pl.pallas_call(
        functools.partial(_mlstm_kernel, cs=cs, rows=rows, valid=valid, zero_init=zero_init, H=H, DM=DM, nc=nc),
        grid=(nseq, nc),
        in_specs=in_specs,
        out_specs=[blk(0), st_c, st_v, st_v],
        out_shape=[jax.ShapeDtypeStruct((M, WM), out_dtype),
                   jax.ShapeDtypeStruct((nseq, H, DM, DM), F32),
                   jax.ShapeDtypeStruct((nseq, H, DM), F32),
                   jax.ShapeDtypeStruct((nseq, H, DM), F32)],
        scratch_shapes=[pltpu.VMEM((H, DM, DM), F32), pltpu.VMEM((H, DM), F32), pltpu.VMEM((H, DM), F32)],
        compiler_params=_cparams(("arbitrary", "arbitrary")),
        name="mlstm",
    )(*args)


def _largest_divisor(n, cap, mult):
    best = None
    for t in range(mult, min(n, cap) + 1, mult):
        if n % t == 0:
            best = t
    assert best is not None, (n, cap, mult)
    return best


def _tiles(L, D, F):
    return dict(
        attn_t=_largest_divisor(L, 512, 128),
        attn_heads=4,
        mlstm_cs=_largest_divisor(L, 256, 128),
        inproj_bm=_largest_divisor(L, 512, 8),
        out_bm=_largest_divisor(L, 512, 8),
        up_bm=_largest_divisor(L, 1024, 8),
        up_bn=_largest_divisor(F, 512, 128),
        up_bn_sample=_largest_divisor(F, 704, 128),
        down_bm=_largest_divisor(L, 512, 8),
        down_bk=_largest_divisor(F, 1408, 128),
        pages_per_step=16,
    )


def kernel(x_prompt, x_sample, cache_k, cache_v, page_table, state_C, state_n, state_m, state_conv, rel_bias, w_in, b_i, b_f, b_o, lambda_q1, lambda_k1, lambda_q2, lambda_k2, subln_w, mhn_w, w_out, ln1_g, ln1_b, w_up, conv_w, conv_b, w_down, ln2_g, ln2_b):
    B, L, D = x_prompt.shape
    BS, LS, _ = x_sample.shape
    DEPTH, NPOOL, P, HA, DK = cache_k.shape
    DH = DK // 2
    WA = HA * DK
    HM, DM = state_C.shape[2], state_C.shape[3]
    WM = HM * DM
    F = w_down.shape[1]
    NP = page_table.shape[1]
    NQ = SAMPLE_ROWS
    assert WA == WM and w_in.shape[2] == 3 * WA + 4 * WM + 2 * HM
    assert LS <= NQ and LS >= CONV_W - 1 and P >= _FAR and HA == 8 and HM == 8
    t = _tiles(L, D, F)
    T = t["attn_t"]
    assert T >= _FAR
    G = min(t["pages_per_step"], NP)
    assert NP % G == 0
    alpha = (2 * DEPTH) ** 0.25

    pbias = _prompt_bias(rel_bias, T)
    dbias, nbias = _decode_bias(rel_bias, P, G)
    cache_k2 = cache_k.reshape(DEPTH * NPOOL, P * HA, DK)
    cache_v2 = cache_v.reshape(DEPTH * NPOOL, P * HA, DK)
    zconv = jnp.zeros((B, CONV_W - 1, 2 * F), F32)
    sample_cs = 128

    xp32 = x_prompt.reshape(B * L, D)
    xs32 = jnp.pad(x_sample, ((0, 0), (0, NQ - LS), (0, 0))).reshape(BS * NQ, D)
    xp16, xs16 = xp32, xs32
    wi16 = w_in.astype(BF16)
    wo16 = w_out.astype(BF16)
    wd16 = w_down.astype(BF16)

    outs = {k: [] for k in ("kp", "vp", "ks", "vs", "Cp", "np", "mp", "Cs", "ns", "ms", "cp", "cs")}
    for l in range(DEPTH):
        lam0 = 0.8 - 0.6 * math.exp(-0.3 * l)
        lam_init = jnp.full((1,), lam0, F32)
        lam_vecs = jnp.stack([lambda_q1[l], lambda_k1[l], lambda_q2[l], lambda_k2[l]])
        wg = jnp.pad(w_in[l, :, 3 * WA + 4 * WM:], ((0, 0), (0, 128 - 2 * HM))).astype(BF16)
        gate_bias = jnp.concatenate([b_i[l], b_f[l]]).reshape(2 * HM, 1)
        page_idx = (page_table + l * NPOOL).reshape(-1).astype(jnp.int32)

        pb, k32, v32, om32, gates = _inproj(xp16, wi16, l, wg, bm=t["inproj_bm"], rows_per_seq=L, pb_dtype=BF16,
                                            W=WA, H=HA, NG=2 * HM)
        oa = _prompt_attention(pb, rel_bias, pbias, lam_vecs, subln_w[l], lam_init, B=B, L=L, H=HA, DH=DH, T=T,
                               HG=t["attn_heads"])
        om, Cn, nn, mn = _mlstm(gates, gate_bias, pb, om32, b_o[l], mhn_w[l], None, nseq=B, cs=t["mlstm_cs"],
                                rows=t["mlstm_cs"], valid=t["mlstm_cs"], H=HM, DM=DM, out_dtype=BF16)
        h32, h16 = _outproj_ln(oa, om, wo16, l, xp32, ln1_g[l], ln1_b[l], alpha=alpha, bm=t["out_bm"])
        act, ta, tg = _upconv(h16, w_up, l, conv_w[l], conv_b[l], zconv, bm=t["up_bm"], bn=t["up_bn"],
                              seq_rows=L, act_dtype=BF16)
        xp32, xp16 = _down_ln(act, wd16, l, h32, ln2_g[l], ln2_b[l], alpha=alpha, bm=t["down_bm"], bk=t["down_bk"])
        outs["kp"].append(k32.reshape(B, L, HA, DK))
        outs["vp"].append(v32.reshape(B, L, HA, DK))
        outs["Cp"].append(Cn)
        outs["np"].append(nn)
        outs["mp"].append(mn[:, :, 0])
        outs["cp"].append(jnp.concatenate([ta[:, 8 - (CONV_W - 1):], tg[:, 8 - (CONV_W - 1):]], axis=-1))

        MS = BS * NQ
        pb, k32, v32, om32, gates = _inproj(xs16, wi16, l, wg, bm=MS, rows_per_seq=MS, pb_dtype=F32,
                                            W=WA, H=HA, NG=2 * HM)
        oa = _decode_attention(pb, cache_k2, cache_v2, page_idx, dbias, nbias, lam_vecs, subln_w[l], lam_init,
                               BS=BS, NP=NP, G=G, P=P, H=HA, DH=DH)
        gs = gates.reshape(2 * HM, BS, NQ).transpose(1, 0, 2)
        gs = jnp.pad(gs, ((0, 0), (0, 0), (0, sample_cs - NQ)))
        state = (state_C[l], state_n[l], jnp.broadcast_to(state_m[l][:, :, None], (BS, HM, DM)))
        om, Cn, nn, mn = _mlstm(gs, gate_bias, pb, om32, b_o[l], mhn_w[l], state, nseq=BS, cs=sample_cs,
                                rows=NQ, valid=LS, H=HM, DM=DM, out_dtype=F32)
        h32, h16 = _outproj_ln(oa, om, wo16, l, xs32, ln1_g[l], ln1_b[l], alpha=alpha, bm=MS)
        sc = state_conv[l]
        c0x = jnp.stack([jnp.pad(sc[:, CONV_W - 2:], ((0, 0), (0, NQ - 1), (0, 0))),
                         jnp.pad(sc, ((0, 0), (0, NQ - (CONV_W - 1)), (0, 0)))]).reshape(2, MS, 2 * F)
        act, ta, tg = _upconv(h16, w_up, l, conv_w[l], conv_b[l], c0x, bm=MS, bn=t["up_bn_sample"],
                              seq_rows=NQ, act_dtype=F32)
        ta, tg = ta.reshape(BS, NQ, F), tg.reshape(BS, NQ, F)
        xs32, xs16 = _down_ln(act, wd16, l, h32, ln2_g[l], ln2_b[l], alpha=alpha, bm=MS, bk=t["down_bk"])
        outs["ks"].append(k32.reshape(BS, NQ, HA, DK)[:, :LS])
        outs["vs"].append(v32.reshape(BS, NQ, HA, DK)[:, :LS])
        outs["Cs"].append(Cn)
        outs["ns"].append(nn)
        outs["ms"].append(mn[:, :, 0])
        outs["cs"].append(jnp.concatenate([ta[:, LS - (CONV_W - 1):LS], tg[:, LS - (CONV_W - 1):LS]], axis=-1))

    st = lambda k: jnp.stack(outs[k])
    return (xp32.reshape(B, L, D), xs32.reshape(BS, NQ, D)[:, :LS],
            st("kp"), st("vp"), st("ks"), st("vs"),
            st("Cp"), st("np"), st("mp"), st("Cs"), st("ns"), st("ms"),
            st("cp"), st("cs"))
```
